```python
import jax, jax.numpy as jnp
from jax import lax
import numpy as np

D_MODEL = 1024
BATCH = 8
SEQ = 8192
DEPTH = 1
DEC_BATCH = 128
DEC_SEQ = 8
PAST_LEN = 8192
PAGE_SIZE = 128

GLA_HEADS = 4
GLA_DK = 64
GLA_DV = D_MODEL // 2 // GLA_HEADS
GLA_RANK = 16
GLA_TEMP = 16.0
GLA_CHUNK = 64
SWA_HEADS = 8
SWA_HD = D_MODEL // 2 // SWA_HEADS
DILATED_PATTERNS = ((128, 1), (512, 4), (2048, 16))
WIN_MAX = 2048
MIX_WIDTH = GLA_HEADS * GLA_DV + SWA_HEADS * SWA_HD
D_FF = -(-8 * D_MODEL // (3 * 256)) * 256
EPS = 1e-6
PROJ_SIZES = (GLA_HEADS * GLA_DK, GLA_HEADS * GLA_DK, GLA_HEADS * GLA_DV, GLA_HEADS * GLA_DV,
              GLA_RANK, SWA_HEADS * SWA_HD, SWA_HEADS * SWA_HD, SWA_HEADS * SWA_HD)
PROJ_WIDTH = sum(PROJ_SIZES)
SPLIT_IDX = tuple(int(c) for c in np.cumsum(PROJ_SIZES)[:-1])

kernel_name = 'hymba_gla_dilated_swa_decode_step'

F32 = jnp.float32


def rmsnorm(x, g):
    xf = x.astype(F32)
    y = xf * lax.rsqrt(jnp.mean(xf * xf, axis=-1, keepdims=True) + EPS) * g.astype(F32)
    return y.astype(x.dtype)


def alibi_slopes(n):
    return jnp.exp2(-8.0 * (jnp.arange(n, dtype=F32) + 1.0) / n)


def gla_chunked(q, k, v, log_a, s0, chunk):
    B, T, H, DK = q.shape
    DV = v.shape[-1]
    nc = T // chunk
    def to_chunks(a):
        return jnp.moveaxis(a.astype(F32).reshape(B, nc, chunk, H, a.shape[-1]), 1, 0)
    xs = (to_chunks(q), to_chunks(k), to_chunks(v), to_chunks(log_a))
    causal = jnp.tril(jnp.ones((chunk, chunk), dtype=bool))

    def step(S, inp):
        qc, kc, vc, ac = inp
        b = jnp.cumsum(ac, axis=1)
        diff = b[:, :, None] - b[:, None, :]
        decay = jnp.exp(jnp.where(causal[None, :, :, None, None], diff, -jnp.inf))
        A = jnp.einsum('bthk,bshk,btshk->bhts', qc, kc, decay)
        o_intra = jnp.einsum('bhts,bshv->bthv', A, vc)
        o_inter = jnp.einsum('bthk,bhkv->bthv', qc * jnp.exp(b), S)
        bl = b[:, -1]
        S_new = jnp.exp(bl)[..., None] * S + jnp.einsum(
            'bshk,bshv->bhkv', kc * jnp.exp(bl[:, None] - b), vc)
        return S_new, o_intra + o_inter

    S, o = lax.scan(step, s0.astype(F32), xs)
    o = jnp.moveaxis(o, 0, 1).reshape(B, T, H, DV)
    return S, o


def combine_patterns(parts):
    ms = jnp.stack([p[0] for p in parts])
    M = jnp.max(ms, axis=0)
    sc = jnp.exp(ms - M)
    den = jnp.sum(sc * jnp.stack([p[1] for p in parts]), axis=0)
    num = jnp.sum(sc[..., None] * jnp.stack([p[2] for p in parts]), axis=0)
    return num / den[..., None]


def dilated_prompt(q, k, v):
    B, T, H, E = q.shape
    slopes = alibi_slopes(H)
    parts = []
    for (W, d) in DILATED_PATTERNS:
        n = W // d
        L = T // d
        nb = -(-L // n)
        Lp = nb * n
        def to_blocks(a):
            a = a.reshape(B, L, d, H, E).transpose(0, 2, 1, 3, 4)
            a = jnp.pad(a, ((0, 0), (0, 0), (0, Lp - L), (0, 0), (0, 0)))
            return a.reshape(B, d, nb, n, H, E)
        qb, kb, vb = to_blocks(q), to_blocks(k), to_blocks(v)
        def with_prev(a):
            prev = jnp.pad(a[:, :, :-1], ((0, 0), (0, 0), (1, 0), (0, 0), (0, 0), (0, 0)))
            return jnp.concatenate([prev, a], axis=3)
        kk, vv = with_prev(kb), with_prev(vb)
        s = jnp.einsum('brnqhe,brnkhe->brnhqk', qb, kk, preferred_element_type=F32)
        i = jnp.arange(n)[:, None]
        j = jnp.arange(2 * n)[None, :]
        stp = i + n - j
        band = (stp >= 0) & (stp <= n)
        valid_blk = (jnp.arange(nb)[:, None, None] > 0) | (j[None] >= n)
        valid = band[None] & valid_blk
        bias = -slopes[:, None, None] * (stp * d).astype(F32)[None]
        s = jnp.where(valid[None, None, :, None], s + bias, -jnp.inf)
        m = jnp.max(s, axis=-1)
        p = jnp.exp(s - m[..., None])
        den = jnp.sum(p, axis=-1)
        o = jnp.einsum('brnhqk,brnkhe->brnqhe', p.astype(vv.dtype), vv, preferred_element_type=F32)
        o = o.reshape(B, d, Lp, H, E)[:, :, :L].transpose(0, 2, 1, 3, 4).reshape(B, T, H, E)
        def back(a):
            a = a.transpose(0, 1, 2, 4, 3).reshape(B, d, Lp, H)[:, :, :L]
            return a.transpose(0, 2, 1, 3).reshape(B, T, H)
        parts.append((back(m), back(den), o))
    return combine_patterns(parts)


def dilated_sample(q, k_new, v_new, k_buf, v_buf):
    Bd, Ts, H, E = q.shape
    Wb = k_buf.shape[1]
    slopes = alibi_slopes(H)
    kall = jnp.concatenate([k_buf.astype(k_new.dtype), k_new], axis=1)
    vall = jnp.concatenate([v_buf.astype(v_new.dtype), v_new], axis=1)
    parts = []
    for (W, d) in DILATED_PATTERNS:
        n = W // d
        steps = jnp.arange(n + 1)
        idx = Wb + jnp.arange(Ts)[:, None] - steps[None, :] * d
        valid = idx >= 0
        idxc = jnp.clip(idx, 0, None)
        kg = kall[:, idxc]
        vg = vall[:, idxc]
        s = jnp.einsum('bqhe,bqjhe->bhqj', q, kg, preferred_element_type=F32)
        s = s - slopes[:, None, None] * (steps * d).astype(F32)[None, None, :]
        s = jnp.where(valid[None, None], s, -jnp.inf)
        m = jnp.max(s, axis=-1)
        p = jnp.exp(s - m[..., None])
        den = jnp.sum(p, axis=-1)
        o = jnp.einsum('bhqj,bqjhe->bqhe', p.astype(vg.dtype), vg, preferred_element_type=F32)
        parts.append((m.transpose(0, 2, 1), den.transpose(0, 2, 1), o))
    return combine_patterns(parts)


def decoder_layer(x, s0, k_buf, v_buf, w_in, w_gup, b_g, g_mix, g_gla, w_out,
                  g_ffn, w_fg, w_fu, w_fd):
    B, T, _ = x.shape
    xn = rmsnorm(x, g_mix)
    proj = xn @ w_in
    qg, kg, vg, r, lr, qs, ks, vs = jnp.split(proj, SPLIT_IDX, axis=-1)
    qg = qg.reshape(B, T, GLA_HEADS, GLA_DK) * (GLA_DK ** -0.5)
    kg = kg.reshape(B, T, GLA_HEADS, GLA_DK)
    vg = vg.reshape(B, T, GLA_HEADS, GLA_DV)
    log_a = jax.nn.log_sigmoid((lr @ w_gup + b_g).astype(F32)).reshape(B, T, GLA_HEADS, GLA_DK) / GLA_TEMP
    qs = qs.reshape(B, T, SWA_HEADS, SWA_HD) * (SWA_HD ** -0.5)
    ks = ks.reshape(B, T, SWA_HEADS, SWA_HD)
    vs = vs.reshape(B, T, SWA_HEADS, SWA_HD)
    if k_buf is None:
        s0 = jnp.zeros((B, GLA_HEADS, GLA_DK, GLA_DV), F32)
        chunk = GLA_CHUNK
        o_s = dilated_prompt(qs, ks, vs)
        start = max(T - WIN_MAX, 0)
        k_keep, v_keep = ks[:, start:], vs[:, start:]
    else:
        chunk = T
        o_s = dilated_sample(qs, ks, vs, k_buf, v_buf)
        k_keep, v_keep = ks, vs
    S, o_g = gla_chunked(qg, kg, vg, log_a, s0, chunk)
    o_g = rmsnorm(o_g.astype(x.dtype), g_gla.reshape(GLA_HEADS, GLA_DV)).reshape(B, T, -1) * jax.nn.silu(r)
    mix = jnp.concatenate([o_g, o_s.reshape(B, T, -1).astype(x.dtype)], axis=-1) @ w_out
    h = x + mix
    hn = rmsnorm(h, g_ffn)
    y = h + (jax.nn.silu(hn @ w_fg) * (hn @ w_fu)) @ w_fd
    return y, S, k_keep, v_keep


def setup_inputs(seed: int = 0) -> dict:
    key = jax.random.key(seed)
    ks = jax.random.split(key, 20)
    def nrm(k, shape, scale):
        return jax.random.normal(k, shape, F32) * scale
    wb = min(WIN_MAX, PAST_LEN)
    return {
        'x_prompt': nrm(ks[0], (BATCH, SEQ, D_MODEL), 1.0),
        'x_sample': nrm(ks[1], (DEC_BATCH, DEC_SEQ, D_MODEL), 1.0),
        'state_gla': nrm(ks[2], (DEPTH, DEC_BATCH, GLA_HEADS, GLA_DK, GLA_DV), 1.0),
        'cache_swa_k': nrm(ks[3], (DEPTH, DEC_BATCH, wb, SWA_HEADS, SWA_HD), 1.0),
        'cache_swa_v': nrm(ks[4], (DEPTH, DEC_BATCH, wb, SWA_HEADS, SWA_HD), 1.0),
        'w_in': nrm(ks[5], (DEPTH, D_MODEL, PROJ_WIDTH), D_MODEL ** -0.5),
        'w_gate_up': nrm(ks[6], (DEPTH, GLA_RANK, GLA_HEADS * GLA_DK), GLA_RANK ** -0.5),
        'b_gate': nrm(ks[7], (DEPTH, GLA_HEADS * GLA_DK), 0.1),
        'g_mix_norm': 1.0 + nrm(ks[8], (DEPTH, D_MODEL), 0.02),
        'g_gla_norm': 1.0 + nrm(ks[9], (DEPTH, GLA_HEADS * GLA_DV), 0.02),
        'w_out': nrm(ks[10], (DEPTH, MIX_WIDTH, D_MODEL), MIX_WIDTH ** -0.5),
        'g_ffn_norm': 1.0 + nrm(ks[11], (DEPTH, D_MODEL), 0.02),
        'w_ffn_gate': nrm(ks[12], (DEPTH, D_MODEL, D_FF), D_MODEL ** -0.5),
        'w_ffn_up': nrm(ks[13], (DEPTH, D_MODEL, D_FF), D_MODEL ** -0.5),
        'w_ffn_down': nrm(ks[14], (DEPTH, D_FF, D_MODEL), D_FF ** -0.5),
        'g_final': 1.0 + nrm(ks[15], (D_MODEL,), 0.02),
    }


def reference(x_prompt, x_sample, state_gla, cache_swa_k, cache_swa_v, w_in, w_gate_up, b_gate,
              g_mix_norm, g_gla_norm, w_out, g_ffn_norm, w_ffn_gate, w_ffn_up, w_ffn_down, g_final):
    yp, ys = x_prompt, x_sample
    sp_l, ss_l, kp_l, vp_l, ksn_l, vsn_l = [], [], [], [], [], []
    for l in range(DEPTH):
        wl = (w_in[l], w_gate_up[l], b_gate[l], g_mix_norm[l], g_gla_norm[l], w_out[l],
              g_ffn_norm[l], w_ffn_gate[l], w_ffn_up[l], w_ffn_down[l])
        yp, sp, kp, vp = decoder_layer(yp, None, None, None, *wl)
        ys, ss, ksn, vsn = decoder_layer(ys, state_gla[l], cache_swa_k[l], cache_swa_v[l], *wl)
        sp_l.append(sp); ss_l.append(ss); kp_l.append(kp); vp_l.append(vp)
        ksn_l.append(ksn); vsn_l.append(vsn)
    y_prompt = rmsnorm(yp, g_final)
    y_sample = rmsnorm(ys, g_final)
    return (y_prompt, y_sample, jnp.stack(sp_l), jnp.stack(ss_l), jnp.stack(kp_l), jnp.stack(vp_l),
            jnp.stack(ksn_l), jnp.stack(vsn_l))
```

```python
import functools

import numpy as np
import jax
import jax.numpy as jnp
from jax import lax
from jax.experimental import pallas as pl
from jax.experimental.pallas import tpu as pltpu

F32 = jnp.float32
BF16 = jnp.bfloat16

D_MODEL = 1024
GLA_HEADS = 4
GLA_DK = 64
GLA_DV = 128
GLA_RANK = 16
GLA_TEMP = 16.0
SWA_HEADS = 8
SWA_HD = 64
DILATED_PATTERNS = ((128, 1), (512, 4), (2048, 16))
WIN_MAX = 2048
D_FF = 2816
EPS = 1e-6

GK = GLA_HEADS * GLA_DK
GV = GLA_HEADS * GLA_DV
SW = SWA_HEADS * SWA_HD
LANES = 128
TILE = 2048
RES = 16
BLK = 128
CHUNK = 64
ROWS = 512
SUB = ROWS // RES
NEG = -1e30
FF_CHUNK = 256
C_QG, C_KG, C_VG, C_R, C_QS, C_KS, C_VS, C_LR, C_END = 0, 256, 512, 1024, 1536, 2048, 2560, 3072, 3200
VMEM_LIMIT = 56 * 1024 * 1024


def _rms(x, g):
    ms = jnp.mean(x * x, axis=-1, keepdims=True)
    return x * lax.rsqrt(ms + EPS) * g


def _sigmoid(x):
    return 1.0 / (1.0 + jnp.exp(-x))


def _dot(a, b):
    return jnp.dot(a, b, preferred_element_type=F32)


def _dot_nt(a, b):
    return lax.dot_general(a, b, (((1,), (1,)), ((), ())), preferred_element_type=F32)


def _dot_tn(a, b):
    return lax.dot_general(a, b, (((0,), (0,)), ((), ())), preferred_element_type=F32)


def _const_spec(shape):
    nd = len(shape)
    return pl.BlockSpec(shape, lambda *_: (0,) * nd)


def _resident(shape):
    nd = len(shape)
    return pl.BlockSpec(shape, lambda *_: (0,) * nd, pipeline_mode=pl.Buffered(1))


def _params(n_axes):
    return pltpu.CompilerParams(dimension_semantics=("arbitrary",) * n_axes,
                                vmem_limit_bytes=VMEM_LIMIT)


def _inproj_body(x_ref, w_ref, wg_ref, bg_ref, gm_ref,
                 qg_ref, kg_ref, vg_ref, r_ref, la_ref, *rest, prompt, keep_first):
    xn = _rms(x_ref[...], gm_ref[...]).astype(BF16)

    def proj(lo, hi):
        return _dot(xn, w_ref[:, lo:hi])

    qg_ref[...] = proj(C_QG, C_KG) * (GLA_DK ** -0.5)
    kg_ref[...] = proj(C_KG, C_VG)
    vg_ref[...] = proj(C_VG, C_R).astype(vg_ref.dtype)
    r_ref[...] = proj(C_R, C_QS)
    lr = proj(C_LR, C_END)
    z = _dot(lr.astype(BF16), wg_ref[...]) + bg_ref[...]
    log_sig = -(jnp.maximum(-z, 0.0) + jnp.log1p(jnp.exp(-jnp.abs(z))))
    la_ref[...] = log_sig / GLA_TEMP
    q = proj(C_QS, C_KS) * (SWA_HD ** -0.5)
    k = proj(C_KS, C_VS)
    v = proj(C_VS, C_LR)
    if not prompt:
        qs_ref, ks_ref, vs_ref = rest
        qs_ref[...] = q
        ks_ref[...] = k
        vs_ref[...] = v
        return
    q16_ref, k16_ref, v16_ref, kt_ref, vt_ref, kk_ref, vk_ref, stage = rest
    kt_ref[...] = k.astype(BF16)
    vt_ref[...] = v.astype(BF16)
    for val, ref in ((q, q16_ref), (k, k16_ref), (v, v16_ref)):
        for sl in range(SW // LANES):
            stage[sl] = val[:, sl * LANES:(sl + 1) * LANES]
        for r in range(RES):
            for sl in range(SW // LANES):
                ref[r, :, sl * LANES:(sl + 1) * LANES] = (
                    stage[sl, pl.ds(r, SUB, stride=RES), :].astype(ref.dtype))

    @pl.when(pl.program_id(1) >= keep_first)
    def _():
        kk_ref[...] = k.T
        vk_ref[...] = v.T


def _pack_weights(w_in, w_gup, b_gate, g_mix):
    sizes = (GK, GK, GV, GV, GLA_RANK, SW, SW, SW)
    offs = np.concatenate([[0], np.cumsum(sizes)])
    qg, kg, vg, r, lr, qs, ks, vs = [w_in[:, offs[i]:offs[i + 1]] for i in range(8)]
    lr = jnp.pad(lr, ((0, 0), (0, LANES - GLA_RANK)))
    w = jnp.concatenate([qg, kg, vg, r, qs, ks, vs, lr], axis=1).astype(BF16)
    wg = jnp.pad(w_gup, ((0, LANES - GLA_RANK), (0, 0))).astype(BF16)
    return w, wg, b_gate.reshape(1, GK), g_mix.reshape(1, D_MODEL)


def _inproj_prompt(x, w, wg, bg, gm):
    B, T, _ = x.shape
    nt, per = T // TILE, TILE // ROWS
    n = T // ROWS
    keep_first = n - WIN_MAX // ROWS
    tok = lambda c: pl.BlockSpec((None, ROWS, c), lambda b, i: (b, i, 0))
    slab = pl.BlockSpec((None, None, RES, SUB, SW), lambda b, i: (b, i // per, 0, i % per, 0))
    keep = pl.BlockSpec((None, SW, ROWS), lambda b, i: (b, 0, jnp.maximum(i - keep_first, 0)))
    tshape = lambda c, dt: jax.ShapeDtypeStruct((B, T, c), dt)
    sshape = lambda dt: jax.ShapeDtypeStruct((B, nt, RES, BLK, SW), dt)
    kshape = jax.ShapeDtypeStruct((B, SW, WIN_MAX), F32)
    return pl.pallas_call(
        functools.partial(_inproj_body, prompt=True, keep_first=keep_first),
        grid=(B, n),
        in_specs=[tok(D_MODEL), _resident(w.shape), _const_spec(wg.shape),
                  _const_spec(bg.shape), _const_spec(gm.shape)],
        out_specs=(tok(GK), tok(GK), tok(GV), tok(GV), tok(GK),
                   slab, slab, slab, tok(SW), tok(SW), keep, keep),
        out_shape=(tshape(GK, F32), tshape(GK, F32), tshape(GV, BF16), tshape(GV, F32),
                   tshape(GK, F32), sshape(F32), sshape(BF16), sshape(BF16),
                   tshape(SW, BF16), tshape(SW, BF16), kshape, kshape),
        scratch_shapes=[pltpu.VMEM((SW // LANES, ROWS, LANES), F32)],
        compiler_params=_params(2),
        name="inproj_prompt",
    )(x, w, wg, bg, gm)


def _inproj_sample(x, w, wg, bg, gm):
    N = x.shape[0]
    spec = lambda c: pl.BlockSpec((ROWS, c), lambda i: (i, 0))
    shp = lambda c: jax.ShapeDtypeStruct((N, c), F32)
    return pl.pallas_call(
        functools.partial(_inproj_body, prompt=False, keep_first=0),
        grid=(N // ROWS,),
        in_specs=[spec(D_MODEL), _resident(w.shape), _const_spec(wg.shape),
                  _const_spec(bg.shape), _const_spec(gm.shape)],
        out_specs=(spec(GK), spec(GK), spec(GV), spec(GV), spec(GK), spec(SW), spec(SW), spec(SW)),
        out_shape=(shp(GK), shp(GK), shp(GV), shp(GV), shp(GK), shp(SW), shp(SW), shp(SW)),
        compiler_params=_params(1),
        name="inproj_sample",
    )(x, w, wg, bg, gm)


def _gla_chunk(q, k, v, la, r, g, seg, states):
    C = q.shape[0]
    nseg = C // seg
    rows = lax.broadcasted_iota(jnp.int32, (C, GK), 0)
    rin = rows & (seg - 1)
    b_inc = la
    d = 1
    while d < seg:
        b_inc = b_inc + jnp.where(rin >= d, pltpu.roll(b_inc, d, 0), 0.0)
        d *= 2
    b_exc = jnp.where(rin >= 1, pltpu.roll(b_inc, 1, 0), 0.0)

    ti = lax.broadcasted_iota(jnp.int32, (C, C), 0)
    si = lax.broadcasted_iota(jnp.int32, (C, C), 1)
    lane = lax.broadcasted_iota(jnp.int32, (C, LANES), 1)
    head_lo = lane < GLA_DK
    row_c = lax.broadcasted_iota(jnp.int32, (C, LANES), 0)

    def pair(a, p):
        return a[:, p * LANES:(p + 1) * LANES]

    def head_mask(hp):
        return head_lo if hp == 0 else jnp.logical_not(head_lo)

    A = [jnp.zeros((C, C), F32) for _ in range(GLA_HEADS)]

    def add_level(qs, ks, mask):
        for p in range(2):
            kp = pair(ks, p).astype(BF16)
            qp = pair(qs, p)
            for hp in range(2):
                qm = jnp.where(head_mask(hp), qp, 0.0).astype(BF16)
                A[2 * p + hp] = A[2 * p + hp] + jnp.where(mask, _dot_nt(qm, kp), 0.0)

    add_level(q, k, ti == si)
    start, end = b_exc, b_inc
    h, sh = 1, 0
    while h < seg:
        mask = jnp.logical_and(((ti ^ si) >> sh) == 1, si < ti)
        add_level(q * jnp.exp(b_inc - start), k * jnp.exp(end - b_inc), mask)
        upper = (rin & (2 * h - 1)) >= h
        start = jnp.where(upper, pltpu.roll(start, h, 0), start)
        end = jnp.where(upper, end, pltpu.roll(end, C - h, 0))
        h, sh = 2 * h, sh + 1
    q_int = q * jnp.exp(b_inc - start)
    k_upd = k * jnp.exp(end - b_inc)

    def seg_rows(j):
        return jnp.logical_and(row_c >= j * seg, row_c < (j + 1) * seg)

    outs = []
    for p in range(2):
        qp = pair(q_int, p)
        for hp in range(2):
            hd = 2 * p + hp
            vh = v[:, hd * GLA_DV:(hd + 1) * GLA_DV]
            o = _dot(A[hd].astype(BF16), vh)
            for j in range(nseg):
                m = head_mask(hp) if nseg == 1 else jnp.logical_and(head_mask(hp), seg_rows(j))
                o = o + _dot(jnp.where(m, qp, 0.0).astype(BF16), states[j][p].astype(BF16))
            gh = g[:, hd * GLA_DV:(hd + 1) * GLA_DV]
            rh = r[:, hd * GLA_DV:(hd + 1) * GLA_DV]
            outs.append(_rms(o, gh) * (rh * _sigmoid(rh)))

    r128 = lax.broadcasted_iota(jnp.int32, (LANES, LANES), 0)
    c128 = lax.broadcasted_iota(jnp.int32, (LANES, LANES), 1)
    new_states = []
    for j in range(nseg):
        st = []
        for p in range(2):
            kp = pair(k_upd, p)
            if nseg > 1:
                kp = jnp.where(seg_rows(j), kp, 0.0)
            kp = kp.astype(BF16)
            t0 = _dot_tn(kp, v[:, (2 * p) * GLA_DV:(2 * p + 1) * GLA_DV])
            t1 = _dot_tn(kp, v[:, (2 * p + 1) * GLA_DV:(2 * p + 2) * GLA_DV])
            t = jnp.where(r128 < GLA_DK, t0, t1)
            e_row = jnp.exp(pair(end, p)[j * seg:j * seg + 1, :])
            e_col = jnp.sum(jnp.where(r128 == c128, jnp.broadcast_to(e_row, (LANES, LANES)), 0.0),
                            axis=1, keepdims=True)
            st.append(e_col * states[j][p] + t)
        new_states.append(st)
    return jnp.concatenate(outs, axis=1), new_states


def _gla_prompt_body(q_ref, k_ref, v_ref, la_ref, r_ref, g_ref, o_ref, s_ref, s_scr, *, n_chunks):
    i = pl.program_id(1)

    @pl.when(i == 0)
    def _():
        s_scr[...] = jnp.zeros_like(s_scr)

    g = g_ref[...]

    def step(c, carry):
        sl = pl.ds(pl.multiple_of(c * CHUNK, CHUNK), CHUNK)
        o, st = _gla_chunk(q_ref[sl, :], k_ref[sl, :], v_ref[sl, :], la_ref[sl, :], r_ref[sl, :],
                           g, CHUNK, [[s_scr[0], s_scr[1]]])
        o_ref[sl, :] = o.astype(o_ref.dtype)
        s_scr[0] = st[0][0]
        s_scr[1] = st[0][1]
        return carry

    lax.fori_loop(0, n_chunks, step, 0)

    @pl.when(i == pl.num_programs(1) - 1)
    def _():
        s_ref[...] = s_scr[...]


def _gla_prompt(qg, kg, vg, la, r, g_gla):
    B, T, _ = qg.shape
    spec = lambda c: pl.BlockSpec((None, ROWS, c), lambda b, i: (b, i, 0))
    og, s = pl.pallas_call(
        functools.partial(_gla_prompt_body, n_chunks=ROWS // CHUNK),
        grid=(B, T // ROWS),
        in_specs=[spec(GK), spec(GK), spec(GV), spec(GK), spec(GV), _const_spec((1, GV))],
        out_specs=(spec(GV), pl.BlockSpec((None, 2, LANES, LANES), lambda b, i: (b, 0, 0, 0))),
        out_shape=(jax.ShapeDtypeStruct((B, T, GV), BF16),
                   jax.ShapeDtypeStruct((B, 2, LANES, LANES), F32)),
        scratch_shapes=[pltpu.VMEM((2, LANES, LANES), F32)],
        compiler_params=_params(2),
        name="gla_prompt",
    )(qg, kg, vg, la, r, g_gla.reshape(1, GV))
    return og, s.reshape(B, GLA_HEADS, GLA_DK, GLA_DV)


def _gla_sample_body(q_ref, k_ref, v_ref, la_ref, r_ref, g_ref, s0_ref, o_ref, s_ref, *, seg):
    nseg = CHUNK // seg
    states = [[s0_ref[j, 0], s0_ref[j, 1]] for j in range(nseg)]
    o, st = _gla_chunk(q_ref[...], k_ref[...], v_ref[...].astype(BF16), la_ref[...], r_ref[...],
                       g_ref[...], seg, states)
    o_ref[...] = o
    for j in range(nseg):
        s_ref[j, 0] = st[j][0]
        s_ref[j, 1] = st[j][1]


def _gla_sample(qg, kg, vg, la, r, g_gla, s0, seg):
    N = qg.shape[0]
    nb = CHUNK // seg
    nseq = N // seg
    spec = lambda c: pl.BlockSpec((CHUNK, c), lambda i: (i, 0))
    sspec = pl.BlockSpec((nb, 2, LANES, LANES), lambda i: (i, 0, 0, 0))
    og, s = pl.pallas_call(
        functools.partial(_gla_sample_body, seg=seg),
        grid=(N // CHUNK,),
        in_specs=[spec(GK), spec(GK), spec(GV), spec(GK), spec(GV), _const_spec((1, GV)), sspec],
        out_specs=(spec(GV), sspec),
        out_shape=(jax.ShapeDtypeStruct((N, GV), F32),
                   jax.ShapeDtypeStruct((nseq, 2, LANES, LANES), F32)),
        compiler_params=_params(1),
        name="gla_sample",
    )(qg, kg, vg, la, r, g_gla.reshape(1, GV), s0.reshape(nseq, 2, LANES, LANES))
    return og, s.reshape(nseq, GLA_HEADS, GLA_DK, GLA_DV)


def _alibi_slopes():
    return np.exp2(-8.0 * (np.arange(SWA_HEADS, dtype=np.float64) + 1.0) / SWA_HEADS)


def _prompt_bias_tables():
    slopes = _alibi_slopes()
    idx = np.arange(BLK)
    q_nat = {16: idx, 4: 4 * (idx % 32) + idx // 32, 1: 16 * (idx % 8) + idx // 8}
    k_nat = {16: idx, 4: 4 * (idx % 32) + idx // 32, 1: idx}
    out = np.empty((3, 2, SWA_HEADS, BLK, 2 * BLK), np.float32)
    for pi, d in enumerate((16, 4, 1)):
        j = np.concatenate([k_nat[d], BLK + k_nat[d]])[None, :]
        stp = q_nat[d][:, None] + BLK - j
        band = (stp >= 0) & (stp <= BLK)
        for first in (0, 1):
            valid = band & ((j >= BLK) if first else True)
            bias = -slopes[:, None, None] * (stp * d).astype(np.float64)[None]
            out[pi, first] = np.where(valid[None], bias, NEG).astype(np.float32)
    return out


def _attend(q, kk_of, vv_of, bias_ref, m_prev, l_prev, a_prev):
    lane = lax.broadcasted_iota(jnp.int32, (BLK, LANES), 1)
    lo = lane < SWA_HD
    m_new, l_new, a_new = [], [], []
    for g in range(SWA_HEADS // 2):
        kk, vv = kk_of(g), vv_of(g)
        qg = q[:, g * LANES:(g + 1) * LANES]
        pv, al = [], []
        for hp in range(2):
            h = 2 * g + hp
            qm = jnp.where(lo if hp == 0 else jnp.logical_not(lo), qg, jnp.zeros_like(qg))
            s = _dot_nt(qm, kk) + bias_ref[h]
            mp = m_prev[h]
            mn = jnp.maximum(mp, jnp.max(s, axis=1, keepdims=True))
            alpha = jnp.exp(mp - mn)
            p = jnp.exp(s - jnp.concatenate([mn, mn], axis=1))
            l_new.append(alpha * l_prev[h] + jnp.sum(p, axis=1, keepdims=True))
            m_new.append(mn)
            pv.append(_dot(p.astype(BF16), vv))
            al.append(alpha)
        a_new.append(jnp.where(lo, al[0], al[1]) * a_prev[g] + jnp.where(lo, pv[0], pv[1]))
    return m_new, l_new, a_new


def _attn_prompt_body(q3_ref, q2_ref, q1_ref,
                      k3p, k3c, v3p, v3c, k2p, k2c, v2p, v2c, k1p, k1c, v1p, v1c,
                      bias_ref, o_ref, m_scr, l_scr, a_scr):
    s = pl.program_id(2)
    nh, ng = SWA_HEADS, SWA_HEADS // 2

    @pl.when(s == 0)
    def _():
        m_scr[...] = jnp.full(m_scr.shape, NEG, F32)
        l_scr[...] = jnp.zeros_like(l_scr)
        a_scr[...] = jnp.zeros_like(a_scr)

    def flat_kv(prev_ref, cur_ref):
        return lambda g: jnp.concatenate(
            [prev_ref[:, g * LANES:(g + 1) * LANES], cur_ref[:, g * LANES:(g + 1) * LANES]], axis=0)

    def slab_kv(prev_ref, cur_ref):
        return lambda g: jnp.concatenate(
            [ref[c, :, g * LANES:(g + 1) * LANES] for ref in (prev_ref, cur_ref) for c in range(4)],
            axis=0)

    @pl.when(s < RES)
    def _():
        r16 = s
        m_prev = [m_scr[h, r16] for h in range(nh)]
        l_prev = [l_scr[h, r16] for h in range(nh)]
        a_prev = [a_scr[g, r16] for g in range(ng)]
        m, l, a = _attend(q3_ref[...].astype(BF16), flat_kv(k3p, k3c), flat_kv(v3p, v3c),
                          bias_ref, m_prev, l_prev, a_prev)
        for h in range(nh):
            m_scr[h, r16] = m[h]
            l_scr[h, r16] = l[h]
        for g in range(ng):
            a_scr[g, r16] = a[g]

    @pl.when(jnp.logical_and(s >= RES, s < 2 * RES))
    def _():
        j = s - RES
        r4, kb = j // 4, j % 4
        rs = pl.ds(pl.multiple_of(kb * 32, 32), 32)
        cat = lambda ref, i: jnp.concatenate([ref[i, 4 * c + r4, rs, :] for c in range(4)], axis=0)
        m_prev = [cat(m_scr, h) for h in range(nh)]
        l_prev = [cat(l_scr, h) for h in range(nh)]
        a_prev = [cat(a_scr, g) for g in range(ng)]
        q = jnp.concatenate([q2_ref[c] for c in range(4)], axis=0).astype(BF16)
        m, l, a = _attend(q, slab_kv(k2p, k2c), slab_kv(v2p, v2c), bias_ref,
                          m_prev, l_prev, a_prev)
        for c in range(4):
            cs = slice(32 * c, 32 * c + 32)
            for h in range(nh):
                m_scr[h, 4 * c + r4, rs, :] = m[h][cs]
                l_scr[h, 4 * c + r4, rs, :] = l[h][cs]
            for g in range(ng):
                a_scr[g, 4 * c + r4, rs, :] = a[g][cs]

    @pl.when(s >= 2 * RES)
    def _():
        j = s - 2 * RES
        rs = pl.ds(pl.multiple_of(j * 8, 8), 8)
        cat = lambda ref, i: ref[i, :, rs, :].reshape(BLK, LANES)
        m_prev = [cat(m_scr, h) for h in range(nh)]
        l_prev = [cat(l_scr, h) for h in range(nh)]
        a_prev = [cat(a_scr, g) for g in range(ng)]
        q = q1_ref[...].reshape(BLK, SW).astype(BF16)
        m, l, a = _attend(q, flat_kv(k1p, k1c), flat_kv(v1p, v1c), bias_ref,
                          m_prev, l_prev, a_prev)
        for h in range(nh):
            m_scr[h, :, rs, :] = m[h].reshape(RES, 8, LANES)
            l_scr[h, :, rs, :] = l[h].reshape(RES, 8, LANES)
        for g in range(ng):
            a_scr[g, :, rs, :] = a[g].reshape(RES, 8, LANES)

    @pl.when(s == 3 * RES - 1)
    def _():
        lane = lax.broadcasted_iota(jnp.int32, (BLK, LANES), 1)
        lo = lane < SWA_HD
        for r16 in range(RES):
            for g in range(ng):
                den = jnp.where(lo, l_scr[2 * g, r16], l_scr[2 * g + 1, r16])
                o_ref[r16, :, g * LANES:(g + 1) * LANES] = (a_scr[g, r16] / den).astype(o_ref.dtype)


def _attn_prompt(q16, k16, v16, kt, vt):
    B, nt = q16.shape[0], q16.shape[1]
    bias = jnp.asarray(_prompt_bias_tables())

    def j3(s): return jnp.clip(s, 0, RES - 1)
    def j2(s): return jnp.clip(s - RES, 0, RES - 1)
    def j1(s): return jnp.clip(s - 2 * RES, 0, RES - 1)

    slab = (None, None, None, BLK, SW)
    quad = (None, None, 4, None, 32, SW)
    q3_spec = pl.BlockSpec(slab, lambda b, a, s: (b, a, j3(s), 0, 0))
    q2_spec = pl.BlockSpec(quad, lambda b, a, s: (b, a, 0, j2(s) // 4, j2(s) % 4, 0))
    q1_spec = pl.BlockSpec((None, None, RES, 8, SW), lambda b, a, s: (b, a, 0, j1(s), 0))
    k3c = pl.BlockSpec(slab, lambda b, a, s: (b, a, j3(s), 0, 0))
    k3p = pl.BlockSpec(slab, lambda b, a, s: (b, jnp.maximum(a - 1, 0), j3(s), 0, 0))
    k2c = pl.BlockSpec(quad, lambda b, a, s: (b, a, 0, j2(s) // 4, j2(s) % 4, 0))

    def k2p_idx(b, a, s):
        kb = j2(s) % 4
        return (b, jnp.maximum(a - (kb == 0).astype(jnp.int32), 0), 0, j2(s) // 4, (kb + 3) % 4, 0)

    k2p = pl.BlockSpec(quad, k2p_idx)
    blk = (None, BLK, SW)
    k1c = pl.BlockSpec(blk, lambda b, a, s: (b, 16 * a + j1(s), 0))
    k1p = pl.BlockSpec(blk, lambda b, a, s: (b, jnp.maximum(16 * a + j1(s) - 1, 0), 0))

    def bias_idx(b, a, s):
        pat = s // RES
        blk_idx = jnp.where(pat == 0, a, jnp.where(pat == 1, 4 * a + j2(s) % 4, 16 * a + j1(s)))
        return (pat, (blk_idx == 0).astype(jnp.int32), 0, 0, 0)

    bias_spec = pl.BlockSpec((None, None, SWA_HEADS, BLK, 2 * BLK), bias_idx)
    split = lambda a: a.reshape(B, nt, 4, 4, BLK, SW)
    return pl.pallas_call(
        _attn_prompt_body,
        grid=(B, nt, 3 * RES),
        in_specs=[q3_spec, q2_spec, q1_spec,
                  k3p, k3c, k3p, k3c, k2p, k2c, k2p, k2c, k1p, k1c, k1p, k1c, bias_spec],
        out_specs=pl.BlockSpec((None, None, RES, BLK, SW), lambda b, a, s: (b, a, 0, 0, 0)),
        out_shape=jax.ShapeDtypeStruct((B, nt, RES, BLK, SW), BF16),
        scratch_shapes=[pltpu.VMEM((SWA_HEADS, RES, BLK, LANES), F32),
                        pltpu.VMEM((SWA_HEADS, RES, BLK, LANES), F32),
                        pltpu.VMEM((SWA_HEADS // 2, RES, BLK, LANES), F32)],
        compiler_params=_params(3),
        name="attn_prompt",
    )(q16, split(q16), q16, k16, k16, v16, v16, split(k16), split(k16), split(v16), split(v16),
      kt, kt, vt, vt, bias)


def _sample_tables(wb, ts):
    slopes = _alibi_slopes()
    ncol = wb + LANES
    t = np.arange(ts)[:, None]
    c = np.arange(ncol)[None, :]
    dist = wb + t - c
    mult = np.zeros((ts, ncol), np.float64)
    for (W, d) in DILATED_PATTERNS:
        mult += (dist >= 0) & (dist % d == 0) & (dist // d <= W // d) & (c < wb + ts)
    bias = np.where(mult[None] > 0, -slopes[:, None, None] * dist[None].astype(np.float64), NEG)
    mult = np.broadcast_to(mult[None], bias.shape)
    shp = (SWA_HEADS * ts, ncol)
    bias = bias.reshape(shp).astype(np.float32)
    mult = mult.reshape(shp).astype(np.float32)
    return bias[:, :wb], mult[:, :wb], bias[:, wb:], mult[:, wb:]


def _attn_sample_body(q_ref, kn_ref, vn_ref, kt_ref, vt_ref, bb_ref, mb_ref, bn_ref, mn_ref, o_ref,
                      *, ts):
    nr = SWA_HEADS * ts
    q = q_ref[...]
    qt = jnp.concatenate([q] * SWA_HEADS, axis=0)
    rh = lax.broadcasted_iota(jnp.int32, (nr, SW), 0) // ts
    ch = lax.broadcasted_iota(jnp.int32, (nr, SW), 1) // SWA_HD
    same = rh == ch
    qbd = jnp.where(same, qt, 0.0).astype(BF16)
    zpad = jnp.zeros((LANES - ts, SW), F32)
    kn = jnp.concatenate([kn_ref[...], zpad], axis=0).astype(BF16)
    vn = jnp.concatenate([vn_ref[...], zpad], axis=0).astype(BF16)
    s_b = _dot(qbd, kt_ref[...].astype(BF16)) + bb_ref[...]
    s_n = _dot_nt(qbd, kn) + bn_ref[...]
    m = jnp.maximum(jnp.max(s_b, axis=1, keepdims=True), jnp.max(s_n, axis=1, keepdims=True))
    p_b = jnp.exp(s_b - m) * mb_ref[...]
    p_n = jnp.exp(s_n - m) * mn_ref[...]
    den = jnp.sum(p_b, axis=1, keepdims=True) + jnp.sum(p_n, axis=1, keepdims=True)
    num = _dot_nt(p_b.astype(BF16), vt_ref[...].astype(BF16)) + _dot(p_n.astype(BF16), vn)
    full = jnp.where(same, num / den, 0.0)
    o = full[0:ts]
    for h in range(1, SWA_HEADS):
        o = o + full[h * ts:(h + 1) * ts]
    o_ref[...] = o


def _attn_sample(qs, kn, vn, kbt, vbt, ts):
    Bd, _, wb = kbt.shape
    for (W, d) in DILATED_PATTERNS:
        assert wb - (W // d) * d >= 0, "window buffer shorter than a pattern's reach"
    tables = [jnp.asarray(t) for t in _sample_tables(wb, ts)]
    row = pl.BlockSpec((ts, SW), lambda b: (b, 0))
    buf = pl.BlockSpec((None, SW, wb), lambda b: (b, 0, 0))
    return pl.pallas_call(
        functools.partial(_attn_sample_body, ts=ts),
        grid=(Bd,),
        in_specs=[row, row, row, buf, buf] + [_const_spec(t.shape) for t in tables],
        out_specs=row,
        out_shape=jax.ShapeDtypeStruct((Bd * ts, SW), F32),
        compiler_params=_params(1),
        name="attn_sample",
    )(qs, kn, vn, kbt, vbt, *tables)


def _ffn_body(og_ref, os_ref, x_ref, wo_ref, wfg_ref, wfu_ref, wfd_ref, gf_ref, gl_ref, y_ref,
              *scratch, os_slabs):
    if os_slabs:
        stage, = scratch
        for r in range(RES):
            for sl in range(SW // LANES):
                stage[sl, pl.ds(r, SUB, stride=RES), :] = (
                    os_ref[r, :, sl * LANES:(sl + 1) * LANES].astype(F32))
        osw = jnp.concatenate([stage[sl] for sl in range(SW // LANES)], axis=1).astype(BF16)
    else:
        osw = os_ref[...].astype(BF16)
    og = og_ref[...].astype(BF16)
    h = x_ref[...] + _dot(og, wo_ref[0:GV, :]) + _dot(osw, wo_ref[GV:GV + SW, :])
    hn = _rms(h, gf_ref[...]).astype(BF16)
    acc = jnp.zeros(h.shape, F32)
    for c in range(D_FF // FF_CHUNK):
        cs = slice(c * FF_CHUNK, (c + 1) * FF_CHUNK)
        gate = _dot(hn, wfg_ref[:, cs])
        up = _dot(hn, wfu_ref[:, cs])
        act = (gate * _sigmoid(gate) * up).astype(BF16)
        acc = acc + _dot(act, wfd_ref[cs, :])
    y_ref[...] = _rms(h + acc, gl_ref[...])


def _ffn_weights(w_out, g_ffn, w_fg, w_fu, w_fd, g_final):
    return (w_out.astype(BF16), w_fg.astype(BF16), w_fu.astype(BF16), w_fd.astype(BF16),
            g_ffn.reshape(1, D_MODEL), g_final.reshape(1, D_MODEL))


def _ffn_prompt(og, o16, x, wts):
    B, T, _ = x.shape
    per = TILE // ROWS
    wo, wfg, wfu, wfd, gf, gl = wts
    tok = lambda c: pl.BlockSpec((None, ROWS, c), lambda b, i: (b, i, 0))
    slab = pl.BlockSpec((None, None, RES, SUB, SW), lambda b, i: (b, i // per, 0, i % per, 0))
    return pl.pallas_call(
        functools.partial(_ffn_body, os_slabs=True),
        grid=(B, T // ROWS),
        in_specs=[tok(GV), slab, tok(D_MODEL),
                  _resident(wo.shape), _resident(wfg.shape), _resident(wfu.shape),
                  _resident(wfd.shape), _const_spec(gf.shape), _const_spec(gl.shape)],
        out_specs=tok(D_MODEL),
        out_shape=jax.ShapeDtypeStruct((B, T, D_MODEL), F32),
        scratch_shapes=[pltpu.VMEM((SW // LANES, ROWS, LANES), F32)],
        compiler_params=_params(2),
        name="ffn_prompt",
    )(og, o16, x, wo, wfg, wfu, wfd, gf, gl)


def _ffn_sample(og, osw, x, wts):
    N = x.shape[0]
    wo, wfg, wfu, wfd, gf, gl = wts
    spec = lambda c: pl.BlockSpec((ROWS, c), lambda i: (i, 0))
    return pl.pallas_call(
        functools.partial(_ffn_body, os_slabs=False),
        grid=(N // ROWS,),
        in_specs=[spec(GV), spec(SW), spec(D_MODEL),
                  _resident(wo.shape), _resident(wfg.shape), _resident(wfu.shape),
                  _resident(wfd.shape), _const_spec(gf.shape), _const_spec(gl.shape)],
        out_specs=spec(D_MODEL),
        out_shape=jax.ShapeDtypeStruct((N, D_MODEL), F32),
        compiler_params=_params(1),
        name="ffn_sample",
    )(og, osw, x, wo, wfg, wfu, wfd, gf, gl)


def _layer_prompt(x, w_in, w_gup, b_gate, g_mix, g_gla, w_out, g_ffn, w_fg, w_fu, w_fd, g_last):
    B, T, _ = x.shape
    assert T % TILE == 0 and T >= WIN_MAX and WIN_MAX % ROWS == 0
    w, wg, bg, gm = _pack_weights(w_in, w_gup, b_gate, g_mix)
    qg, kg, vg, r, la, q16, k16, v16, kt, vt, kk, vk = _inproj_prompt(x, w, wg, bg, gm)
    og, S = _gla_prompt(qg, kg, vg, la, r, g_gla)
    o16 = _attn_prompt(q16, k16, v16, kt, vt)
    y = _ffn_prompt(og, o16, x, _ffn_weights(w_out, g_ffn, w_fg, w_fu, w_fd, g_last))
    keep = lambda a: jnp.transpose(a.reshape(B, SWA_HEADS, SWA_HD, WIN_MAX), (0, 3, 1, 2))
    return y, S, keep(kk), keep(vk)


def _layer_sample(x, s0, kbuf, vbuf, w_in, w_gup, b_gate, g_mix, g_gla, w_out, g_ffn,
                  w_fg, w_fu, w_fd, g_last):
    Bd, Ts, _ = x.shape
    N = Bd * Ts
    assert CHUNK % Ts == 0 and N % ROWS == 0
    w, wg, bg, gm = _pack_weights(w_in, w_gup, b_gate, g_mix)
    qg, kg, vg, r, la, qs, ks, vs = _inproj_sample(x.reshape(N, D_MODEL), w, wg, bg, gm)
    og, S = _gla_sample(qg, kg, vg, la, r, g_gla, s0, Ts)
    wb = kbuf.shape[1]
    feat = lambda a: jnp.transpose(a, (0, 2, 3, 1)).reshape(Bd, SW, wb)
    osw = _attn_sample(qs, ks, vs, feat(kbuf), feat(vbuf), Ts)
    y = _ffn_sample(og, osw, x.reshape(N, D_MODEL),
                    _ffn_weights(w_out, g_ffn, w_fg, w_fu, w_fd, g_last))
    new = lambda a: a.reshape(Bd, Ts, SWA_HEADS, SWA_HD)
    return y.reshape(Bd, Ts, D_MODEL), S, new(ks), new(vs)


def kernel(x_prompt, x_sample, state_gla, cache_swa_k, cache_swa_v, w_in, w_gate_up, b_gate,
           g_mix_norm, g_gla_norm, w_out, g_ffn_norm, w_ffn_gate, w_ffn_up, w_ffn_down, g_final):
    depth = w_in.shape[0]
    assert depth == 1, "the final norm is fused into the single layer's FFN kernel"
    wl = (w_in[0], w_gate_up[0], b_gate[0], g_mix_norm[0], g_gla_norm[0], w_out[0],
          g_ffn_norm[0], w_ffn_gate[0], w_ffn_up[0], w_ffn_down[0], g_final)
    yp, sp, kp, vp = _layer_prompt(x_prompt, *wl)
    ys, ss, ksn, vsn = _layer_sample(x_sample, state_gla[0], cache_swa_k[0], cache_swa_v[0], *wl)
    return (yp, ys, sp[None], ss[None], kp[None], vp[None], ksn[None], vsn[None])
```

```python
import functools

import numpy as np
import jax
import jax.numpy as jnp
from jax import lax
from jax.experimental import pallas as pl
from jax.experimental.pallas import tpu as pltpu

F32 = jnp.float32
BF16 = jnp.bfloat16

D_MODEL = 1024
GLA_HEADS = 4
GLA_DK = 64
GLA_DV = 128
GLA_RANK = 16
GLA_TEMP = 16.0
SWA_HEADS = 8
SWA_HD = 64
DILATED_PATTERNS = ((128, 1), (512, 4), (2048, 16))
WIN_MAX = 2048
D_FF = 2816
EPS = 1e-6

GK = GLA_HEADS * GLA_DK
GV = GLA_HEADS * GLA_DV
SW = SWA_HEADS * SWA_HD
LANES = 128
TILE = 2048
RES = 16
BLK = 128
CHUNK = 64
ROWS = 512
SUB = ROWS // RES
NEG = -1e30
FF_CHUNK = 256
C_QG, C_KG, C_VG, C_R, C_QS, C_KS, C_VS, C_LR, C_END = 0, 256, 512, 1024, 1536, 2048, 2560, 3072, 3200
VMEM_LIMIT = 56 * 1024 * 1024


def _rms(x, g):
    ms = jnp.mean(x * x, axis=-1, keepdims=True)
    return x * lax.rsqrt(ms + EPS) * g


def _sigmoid(x):
    return 1.0 / (1.0 + jnp.exp(-x))


def _dot(a, b):
    return jnp.dot(a, b, preferred_element_type=F32)


def _dot_nt(a, b):
    return lax.dot_general(a, b, (((1,), (1,)), ((), ())), preferred_element_type=F32)


def _dot_tn(a, b):
    return lax.dot_general(a, b, (((0,), (0,)), ((), ())), preferred_element_type=F32)


def _const_spec(shape):
    nd = len(shape)
    return pl.BlockSpec(shape, lambda *_: (0,) * nd)


def _resident(shape):
    nd = len(shape)
    return pl.BlockSpec(shape, lambda *_: (0,) * nd, pipeline_mode=pl.Buffered(1))


def _params(n_axes):
    return pltpu.CompilerParams(dimension_semantics=("arbitrary",) * n_axes,
                                vmem_limit_bytes=VMEM_LIMIT)


def _inproj_body(x_ref, w_ref, wg_ref, bg_ref, gm_ref,
                 qg_ref, kg_ref, vg_ref, r_ref, la_ref, *rest, prompt, keep_first):
    xn = _rms(x_ref[...], gm_ref[...]).astype(BF16)

    def proj(lo, hi):
        return _dot(xn, w_ref[:, lo:hi])

    def gla_qk():
        qg_ref[...] = proj(C_QG, C_KG) * (GLA_DK ** -0.5)
        kg_ref[...] = proj(C_KG, C_VG)

    def gla_gate():
        lr = proj(C_LR, C_END)
        z = _dot(lr.astype(BF16), wg_ref[...]) + bg_ref[...]
        log_sig = -(jnp.maximum(-z, 0.0) + jnp.log1p(jnp.exp(-jnp.abs(z))))
        la_ref[...] = log_sig / GLA_TEMP

    def gla_rest():
        vg_ref[...] = proj(C_VG, C_R).astype(vg_ref.dtype)
        r_ref[...] = proj(C_R, C_QS)

    gla_gate()
    if not prompt:
        qs_ref, ks_ref, vs_ref = rest
        qs_ref[...] = proj(C_QS, C_KS) * (SWA_HD ** -0.5)
        ks_ref[...] = proj(C_KS, C_VS)
        vs_ref[...] = proj(C_VS, C_LR)
        gla_qk()
        gla_rest()
        return
    q16_ref, k16_ref, v16_ref, kt_ref, vt_ref, kk_ref, vk_ref, stage = rest

    def to_slabs(val, ref, buf):
        for sl in range(SW // LANES):
            stage[buf, sl] = val[:, sl * LANES:(sl + 1) * LANES]
        for r in range(RES):
            for sl in range(SW // LANES):
                ref[r, :, sl * LANES:(sl + 1) * LANES] = (
                    stage[buf, sl, pl.ds(r, SUB, stride=RES), :].astype(ref.dtype))

    k = proj(C_KS, C_VS)
    kt_ref[...] = k.astype(BF16)
    v = proj(C_VS, C_LR)
    to_slabs(k, k16_ref, 0)
    vt_ref[...] = v.astype(BF16)
    q = proj(C_QS, C_KS) * (SWA_HD ** -0.5)
    to_slabs(v, v16_ref, 1)
    gla_qk()
    to_slabs(q, q16_ref, 2)
    gla_rest()

    @pl.when(pl.program_id(1) >= keep_first)
    def _():
        for sl in range(SW // LANES):
            kk_ref[sl * LANES:(sl + 1) * LANES, :] = stage[0, sl].T
            vk_ref[sl * LANES:(sl + 1) * LANES, :] = stage[1, sl].T


def _pack_weights(w_in, w_gup, b_gate, g_mix):
    sizes = (GK, GK, GV, GV, GLA_RANK, SW, SW, SW)
    offs = np.concatenate([[0], np.cumsum(sizes)])
    qg, kg, vg, r, lr, qs, ks, vs = [w_in[:, offs[i]:offs[i + 1]] for i in range(8)]
    lr = jnp.pad(lr, ((0, 0), (0, LANES - GLA_RANK)))
    w = jnp.concatenate([qg, kg, vg, r, qs, ks, vs, lr], axis=1).astype(BF16)
    wg = jnp.pad(w_gup, ((0, LANES - GLA_RANK), (0, 0))).astype(BF16)
    return w, wg, b_gate.reshape(1, GK), g_mix.reshape(1, D_MODEL)


def _inproj_prompt(x, w, wg, bg, gm):
    B, T, _ = x.shape
    nt, per = T // TILE, TILE // ROWS
    n = T // ROWS
    keep_first = n - WIN_MAX // ROWS
    tok = lambda c: pl.BlockSpec((None, ROWS, c), lambda b, i: (b, i, 0))
    slab = pl.BlockSpec((None, None, RES, SUB, SW), lambda b, i: (b, i // per, 0, i % per, 0))
    keep = pl.BlockSpec((None, SW, ROWS), lambda b, i: (b, 0, jnp.maximum(i - keep_first, 0)))
    tshape = lambda c, dt: jax.ShapeDtypeStruct((B, T, c), dt)
    sshape = lambda dt: jax.ShapeDtypeStruct((B, nt, RES, BLK, SW), dt)
    kshape = jax.ShapeDtypeStruct((B, SW, WIN_MAX), F32)
    return pl.pallas_call(
        functools.partial(_inproj_body, prompt=True, keep_first=keep_first),
        grid=(B, n),
        in_specs=[tok(D_MODEL), _resident(w.shape), _const_spec(wg.shape),
                  _const_spec(bg.shape), _const_spec(gm.shape)],
        out_specs=(tok(GK), tok(GK), tok(GV), tok(GV), tok(GK),
                   slab, slab, slab, tok(SW), tok(SW), keep, keep),
        out_shape=(tshape(GK, F32), tshape(GK, F32), tshape(GV, BF16), tshape(GV, F32),
                   tshape(GK, F32), sshape(F32), sshape(BF16), sshape(BF16),
                   tshape(SW, BF16), tshape(SW, BF16), kshape, kshape),
        scratch_shapes=[pltpu.VMEM((3, SW // LANES, ROWS, LANES), F32)],
        compiler_params=_params(2),
        name="inproj_prompt",
    )(x, w, wg, bg, gm)


def _inproj_sample(x, w, wg, bg, gm):
    N = x.shape[0]
    spec = lambda c: pl.BlockSpec((ROWS, c), lambda i: (i, 0))
    shp = lambda c: jax.ShapeDtypeStruct((N, c), F32)
    return pl.pallas_call(
        functools.partial(_inproj_body, prompt=False, keep_first=0),
        grid=(N // ROWS,),
        in_specs=[spec(D_MODEL), _resident(w.shape), _const_spec(wg.shape),
                  _const_spec(bg.shape), _const_spec(gm.shape)],
        out_specs=(spec(GK), spec(GK), spec(GV), spec(GV), spec(GK), spec(SW), spec(SW), spec(SW)),
        out_shape=(shp(GK), shp(GK), shp(GV), shp(GV), shp(GK), shp(SW), shp(SW), shp(SW)),
        compiler_params=_params(1),
        name="inproj_sample",
    )(x, w, wg, bg, gm)


def _gla_masks(C, seg):
    ti = lax.broadcasted_iota(jnp.int32, (C, C), 0)
    si = lax.broadcasted_iota(jnp.int32, (C, C), 1)
    masks = [ti == si]
    h, sh = 1, 0
    while h < seg:
        masks.append(jnp.logical_and(((ti ^ si) >> sh) == 1, si < ti))
        h, sh = 2 * h, sh + 1
    return masks


def _gla_chunk(q, k, v, la, r, g, seg, states, masks):
    C = q.shape[0]
    nseg = C // seg
    rows = lax.broadcasted_iota(jnp.int32, (C, GK), 0)
    rin = rows & (seg - 1)
    b_inc = la
    d = 1
    while d < seg:
        b_inc = b_inc + jnp.where(rin >= d, pltpu.roll(b_inc, d, 0), 0.0)
        d *= 2
    b_exc = jnp.where(rin >= 1, pltpu.roll(b_inc, 1, 0), 0.0)

    lane = lax.broadcasted_iota(jnp.int32, (C, LANES), 1)
    head_lo = lane < GLA_DK
    row_c = lax.broadcasted_iota(jnp.int32, (C, LANES), 0)

    def pair(a, p):
        return a[:, p * LANES:(p + 1) * LANES]

    def by_head(a):
        return jnp.concatenate([jnp.where(head_lo, a, 0.0), jnp.where(head_lo, 0.0, a)], axis=0)

    A = [jnp.zeros((C, C), F32) for _ in range(GLA_HEADS)]

    def add_level(qs, ks, mask):
        for p in range(2):
            a2 = _dot_nt(by_head(pair(qs, p)).astype(BF16), pair(ks, p).astype(BF16))
            A[2 * p] = jnp.where(mask, a2[:C], A[2 * p])
            A[2 * p + 1] = jnp.where(mask, a2[C:], A[2 * p + 1])

    add_level(q, k, masks[0])
    start, end = b_exc, b_inc
    h, lvl = 1, 1
    while h < seg:
        add_level(q * jnp.exp(b_inc - start), k * jnp.exp(end - b_inc), masks[lvl])
        upper = (rin & (2 * h - 1)) >= h
        start = jnp.where(upper, pltpu.roll(start, h, 0), start)
        end = jnp.where(upper, end, pltpu.roll(end, C - h, 0))
        h, lvl = 2 * h, lvl + 1
    q_int = q * jnp.exp(b_inc - start)
    k_upd = k * jnp.exp(end - b_inc)

    def seg_rows(j):
        return jnp.logical_and(row_c >= j * seg, row_c < (j + 1) * seg)

    outs = []
    for p in range(2):
        qp = pair(q_int, p)
        inter = jnp.zeros((2 * C, LANES), F32)
        for j in range(nseg):
            qj = qp if nseg == 1 else jnp.where(seg_rows(j), qp, 0.0)
            inter = inter + _dot(by_head(qj).astype(BF16), states[j][p].astype(BF16))
        for hp in range(2):
            hd = 2 * p + hp
            vh = v[:, hd * GLA_DV:(hd + 1) * GLA_DV]
            o = _dot(A[hd].astype(BF16), vh) + inter[hp * C:(hp + 1) * C]
            gh = g[:, hd * GLA_DV:(hd + 1) * GLA_DV]
            rh = r[:, hd * GLA_DV:(hd + 1) * GLA_DV]
            outs.append(_rms(o, gh) * (rh * _sigmoid(rh)))

    r128 = lax.broadcasted_iota(jnp.int32, (LANES, LANES), 0)
    c128 = lax.broadcasted_iota(jnp.int32, (LANES, LANES), 1)
    new_states = []
    for j in range(nseg):
        st = []
        for p in range(2):
            kp = pair(k_upd, p)
            if nseg > 1:
                kp = jnp.where(seg_rows(j), kp, 0.0)
            kp = kp.astype(BF16)
            t0 = _dot_tn(kp, v[:, (2 * p) * GLA_DV:(2 * p + 1) * GLA_DV])
            t1 = _dot_tn(kp, v[:, (2 * p + 1) * GLA_DV:(2 * p + 2) * GLA_DV])
            t = jnp.where(r128 < GLA_DK, t0, t1)
            e_row = jnp.exp(pair(end, p)[j * seg:j * seg + 1, :])
            e_col = jnp.sum(jnp.where(r128 == c128, jnp.broadcast_to(e_row, (LANES, LANES)), 0.0),
                            axis=1, keepdims=True)
            st.append(e_col * states[j][p] + t)
        new_states.append(st)
    return jnp.concatenate(outs, axis=1), new_states


def _gla_prompt_body(q_ref, k_ref, v_ref, la_ref, r_ref, g_ref, o_ref, s_ref, s_scr, *, n_chunks):
    i = pl.program_id(1)

    @pl.when(i == 0)
    def _():
        s_scr[...] = jnp.zeros_like(s_scr)

    g = g_ref[...]
    masks = _gla_masks(CHUNK, CHUNK)

    def step(c, carry):
        sl = pl.ds(pl.multiple_of(c * CHUNK, CHUNK), CHUNK)
        o, st = _gla_chunk(q_ref[sl, :], k_ref[sl, :], v_ref[sl, :], la_ref[sl, :], r_ref[sl, :],
                           g, CHUNK, [[s_scr[0], s_scr[1]]], masks)
        o_ref[sl, :] = o.astype(o_ref.dtype)
        s_scr[0] = st[0][0]
        s_scr[1] = st[0][1]
        return carry

    lax.fori_loop(0, n_chunks, step, 0, unroll=2)

    @pl.when(i == pl.num_programs(1) - 1)
    def _():
        s_ref[...] = s_scr[...]


def _gla_prompt(qg, kg, vg, la, r, g_gla):
    B, T, _ = qg.shape
    spec = lambda c: pl.BlockSpec((None, ROWS, c), lambda b, i: (b, i, 0))
    og, s = pl.pallas_call(
        functools.partial(_gla_prompt_body, n_chunks=ROWS // CHUNK),
        grid=(B, T // ROWS),
        in_specs=[spec(GK), spec(GK), spec(GV), spec(GK), spec(GV), _const_spec((1, GV))],
        out_specs=(spec(GV), pl.BlockSpec((None, 2, LANES, LANES), lambda b, i: (b, 0, 0, 0))),
        out_shape=(jax.ShapeDtypeStruct((B, T, GV), BF16),
                   jax.ShapeDtypeStruct((B, 2, LANES, LANES), F32)),
        scratch_shapes=[pltpu.VMEM((2, LANES, LANES), F32)],
        compiler_params=_params(2),
        name="gla_prompt",
    )(qg, kg, vg, la, r, g_gla.reshape(1, GV))
    return og, s.reshape(B, GLA_HEADS, GLA_DK, GLA_DV)


def _gla_sample_body(q_ref, k_ref, v_ref, la_ref, r_ref, g_ref, s0_ref, o_ref, s_ref, *, seg):
    nseg = CHUNK // seg
    states = [[s0_ref[j, 0], s0_ref[j, 1]] for j in range(nseg)]
    o, st = _gla_chunk(q_ref[...], k_ref[...], v_ref[...].astype(BF16), la_ref[...], r_ref[...],
                       g_ref[...], seg, states, _gla_masks(CHUNK, seg))
    o_ref[...] = o
    for j in range(nseg):
        s_ref[j, 0] = st[j][0]
        s_ref[j, 1] = st[j][1]


def _gla_sample(qg, kg, vg, la, r, g_gla, s0, seg):
    N = qg.shape[0]
    nb = CHUNK // seg
    nseq = N // seg
    spec = lambda c: pl.BlockSpec((CHUNK, c), lambda i: (i, 0))
    sspec = pl.BlockSpec((nb, 2, LANES, LANES), lambda i: (i, 0, 0, 0))
    og, s = pl.pallas_call(
        functools.partial(_gla_sample_body, seg=seg),
        grid=(N // CHUNK,),
        in_specs=[spec(GK), spec(GK), spec(GV), spec(GK), spec(GV), _const_spec((1, GV)), sspec],
        out_specs=(spec(GV), sspec),
        out_shape=(jax.ShapeDtypeStruct((N, GV), F32),
                   jax.ShapeDtypeStruct((nseq, 2, LANES, LANES), F32)),
        compiler_params=_params(1),
        name="gla_sample",
    )(qg, kg, vg, la, r, g_gla.reshape(1, GV), s0.reshape(nseq, 2, LANES, LANES))
    return og, s.reshape(nseq, GLA_HEADS, GLA_DK, GLA_DV)


def _alibi_slopes():
    return np.exp2(-8.0 * (np.arange(SWA_HEADS, dtype=np.float64) + 1.0) / SWA_HEADS)


def _prompt_bias_tables():
    slopes = _alibi_slopes()
    idx = np.arange(BLK)
    q_nat = {16: idx, 4: 4 * (idx % 32) + idx // 32, 1: 16 * (idx % 8) + idx // 8}
    k_nat = {16: idx, 4: 4 * (idx % 32) + idx // 32, 1: idx}
    out = np.empty((3, 2, SWA_HEADS, BLK, 2 * BLK), np.float32)
    for pi, d in enumerate((16, 4, 1)):
        j = np.concatenate([k_nat[d], BLK + k_nat[d]])[None, :]
        stp = q_nat[d][:, None] + BLK - j
        band = (stp >= 0) & (stp <= BLK)
        for first in (0, 1):
            valid = band & ((j >= BLK) if first else True)
            bias = -slopes[:, None, None] * (stp * d).astype(np.float64)[None]
            out[pi, first] = np.where(valid[None], bias, NEG).astype(np.float32)
    return out


def _attend(q, kk, vv, bias_of, prev):
    lane = lax.broadcasted_iota(jnp.int32, (BLK, LANES), 1)
    lo = lane < SWA_HD
    zero = jnp.zeros_like(q)
    q2 = jnp.concatenate([jnp.where(lo, q, zero), jnp.where(lo, zero, q)], axis=0)
    s2 = _dot_nt(q2, kk)
    m_new, l_new, ps, alphas = [], [], [], []
    for hp in range(2):
        s = s2[hp * BLK:(hp + 1) * BLK] + bias_of(hp)
        mc = jnp.max(s, axis=1, keepdims=True)
        if prev is None:
            mn = jnp.broadcast_to(mc, (BLK, LANES))
        else:
            mn = jnp.maximum(prev[0][hp], mc)
            alphas.append(jnp.exp(prev[0][hp] - mn))
        p = jnp.exp(s - jnp.concatenate([mn, mn], axis=1))
        rs = jnp.sum(p, axis=1, keepdims=True)
        if prev is None:
            l_new.append(jnp.broadcast_to(rs, (BLK, LANES)))
        else:
            l_new.append(alphas[hp] * prev[1][hp] + rs)
        m_new.append(mn)
        ps.append(p.astype(BF16))
    pv2 = _dot(jnp.concatenate(ps, axis=0), vv)
    acc = jnp.where(lo, pv2[:BLK], pv2[BLK:])
    if prev is not None:
        acc = jnp.where(lo, alphas[0], alphas[1]) * prev[2] + acc
    return m_new, l_new, acc


def _attn_prompt_body(q_ref, kc_ref, vc_ref, kp_ref, vp_ref, ktc_ref, vtc_ref, ktp_ref, vtp_ref,
                      bias_ref, o_ref, m_scr, l_scr, a_scr, tk_scr, tv_scr):
    tile0 = (pl.program_id(2) == 0).astype(jnp.int32)
    tk_scr[0:BLK, :] = ktp_ref[...]
    tk_scr[BLK:, :] = ktc_ref[...]
    tv_scr[0:BLK, :] = vtp_ref[...]
    tv_scr[BLK:, :] = vtc_ref[...]
    group = 4

    def d16(it, carry):
        for i in range(group):
            r = it * group + i
            kk = jnp.concatenate([kp_ref[r], kc_ref[r]], axis=0)
            vv = jnp.concatenate([vp_ref[r], vc_ref[r]], axis=0)
            m, l, a = _attend(q_ref[r].astype(BF16), kk, vv,
                              lambda hp: bias_ref[0, tile0, hp], None)
            for hp in range(2):
                m_scr[hp, r] = m[hp]
                l_scr[hp, r] = l[hp]
            a_scr[r] = a
        return carry

    for it in range(RES // group):
        d16(it, 0)

    def d4(r4, carry):
        for kb in range(4):
            rows = slice(32 * kb, 32 * kb + 32)

            def keys(cur, prev):
                if kb == 0:
                    before = [prev[4 * c + r4, 96:128, :] for c in range(4)]
                else:
                    before = [cur[4 * c + r4, 32 * kb - 32:32 * kb, :] for c in range(4)]
                return jnp.concatenate(before + [cur[4 * c + r4, rows, :] for c in range(4)], axis=0)

            cat = lambda ref, *i: jnp.concatenate([ref[(*i, 4 * c + r4, rows)] for c in range(4)], axis=0)
            first = tile0 if kb == 0 else 0
            prev = ([cat(m_scr, hp) for hp in range(2)], [cat(l_scr, hp) for hp in range(2)],
                    cat(a_scr))
            m, l, a = _attend(cat(q_ref).astype(BF16), keys(kc_ref, kp_ref), keys(vc_ref, vp_ref),
                              lambda hp: bias_ref[1, first, hp], prev)
            for c in range(4):
                cs = slice(32 * c, 32 * c + 32)
                for hp in range(2):
                    m_scr[hp, 4 * c + r4, rows, :] = m[hp][cs]
                    l_scr[hp, 4 * c + r4, rows, :] = l[hp][cs]
                a_scr[4 * c + r4, rows, :] = a[cs]
        return carry

    for r4 in range(4):
        d4(r4, 0)

    def d1(it, carry):
        for i in range(group):
            j = it * group + i
            rs = slice(j * 8, j * 8 + 8)
            ks = slice(j * BLK, j * BLK + 2 * BLK)
            cat = lambda ref, *idx: ref[(*idx, slice(None), rs, slice(None))].reshape(BLK, LANES)
            first = tile0 if j == 0 else 0
            prev = ([cat(m_scr, hp) for hp in range(2)], [cat(l_scr, hp) for hp in range(2)],
                    cat(a_scr))
            m, l, a = _attend(cat(q_ref).astype(BF16), tk_scr[ks, :], tv_scr[ks, :],
                              lambda hp: bias_ref[2, first, hp], prev)
            for hp in range(2):
                m_scr[hp, :, rs, :] = m[hp].reshape(RES, 8, LANES)
                l_scr[hp, :, rs, :] = l[hp].reshape(RES, 8, LANES)
            a_scr[:, rs, :] = a.reshape(RES, 8, LANES)
        return carry

    for it in range(RES // group):
        d1(it, 0)

    lo = lax.broadcasted_iota(jnp.int32, (BLK, LANES), 1) < SWA_HD
    for r in range(RES):
        o_ref[r] = (a_scr[r] / jnp.where(lo, l_scr[0, r], l_scr[1, r])).astype(o_ref.dtype)


def _attn_prompt(q16, k16, v16, kt, vt):
    B, nt = q16.shape[0], q16.shape[1]
    bias = jnp.asarray(_prompt_bias_tables())
    tile = (None, None, RES, BLK, LANES)
    cur = pl.BlockSpec(tile, lambda g, b, a: (b, a, 0, 0, g))
    prv = pl.BlockSpec(tile, lambda g, b, a: (b, jnp.maximum(a - 1, 0), 0, 0, g))
    tcur = pl.BlockSpec((None, TILE, LANES), lambda g, b, a: (b, a, g))
    tprv = pl.BlockSpec((None, BLK, LANES),
                        lambda g, b, a: (b, jnp.maximum(a * (TILE // BLK) - 1, 0), g))
    bias_spec = pl.BlockSpec((3, 2, 2, BLK, 2 * BLK), lambda g, b, a: (0, 0, g, 0, 0))
    return pl.pallas_call(
        _attn_prompt_body,
        grid=(SWA_HEADS // 2, B, nt),
        in_specs=[cur, cur, cur, prv, prv, tcur, tcur, tprv, tprv, bias_spec],
        out_specs=cur,
        out_shape=jax.ShapeDtypeStruct((B, nt, RES, BLK, SW), BF16),
        scratch_shapes=[pltpu.VMEM((2, RES, BLK, LANES), F32),
                        pltpu.VMEM((2, RES, BLK, LANES), F32),
                        pltpu.VMEM((RES, BLK, LANES), F32),
                        pltpu.VMEM((TILE + BLK, LANES), BF16),
                        pltpu.VMEM((TILE + BLK, LANES), BF16)],
        compiler_params=_params(3),
        name="attn_prompt",
    )(q16, k16, v16, k16, v16, kt, vt, kt, vt, bias)


def _sample_tables(wb, ts):
    slopes = _alibi_slopes()
    ncol = wb + LANES
    t = np.arange(ts)[:, None]
    c = np.arange(ncol)[None, :]
    dist = wb + t - c
    mult = np.zeros((ts, ncol), np.float64)
    for (W, d) in DILATED_PATTERNS:
        mult += (dist >= 0) & (dist % d == 0) & (dist // d <= W // d) & (c < wb + ts)
    bias = np.where(mult[None] > 0, -slopes[:, None, None] * dist[None].astype(np.float64), NEG)
    mult = np.broadcast_to(mult[None], bias.shape)
    shp = (SWA_HEADS * ts, ncol)
    bias = bias.reshape(shp).astype(np.float32)
    mult = mult.reshape(shp).astype(np.float32)
    return bias[:, :wb], mult[:, :wb], bias[:, wb:], mult[:, wb:]


def _attn_sample_body(q_ref, kn_ref, vn_ref, kt_ref, vt_ref, bb_ref, mb_ref, bn_ref, mn_ref, o_ref,
                      *, ts):
    nr = SWA_HEADS * ts
    q = q_ref[...]
    qt = jnp.concatenate([q] * SWA_HEADS, axis=0)
    rh = lax.broadcasted_iota(jnp.int32, (nr, SW), 0) // ts
    ch = lax.broadcasted_iota(jnp.int32, (nr, SW), 1) // SWA_HD
    same = rh == ch
    qbd = jnp.where(same, qt, 0.0).astype(BF16)
    zpad = jnp.zeros((LANES - ts, SW), F32)
    kn = jnp.concatenate([kn_ref[...], zpad], axis=0).astype(BF16)
    vn = jnp.concatenate([vn_ref[...], zpad], axis=0).astype(BF16)
    s_b = _dot(qbd, kt_ref[...].astype(BF16)) + bb_ref[...]
    s_n = _dot_nt(qbd, kn) + bn_ref[...]
    m = jnp.maximum(jnp.max(s_b, axis=1, keepdims=True), jnp.max(s_n, axis=1, keepdims=True))
    p_b = jnp.exp(s_b - m) * mb_ref[...]
    p_n = jnp.exp(s_n - m) * mn_ref[...]
    den = jnp.sum(p_b, axis=1, keepdims=True) + jnp.sum(p_n, axis=1, keepdims=True)
    num = _dot_nt(p_b.astype(BF16), vt_ref[...].astype(BF16)) + _dot(p_n.astype(BF16), vn)
    full = jnp.where(same, num / den, 0.0)
    o = full[0:ts]
    for h in range(1, SWA_HEADS):
        o = o + full[h * ts:(h + 1) * ts]
    o_ref[...] = o


def _attn_sample(qs, kn, vn, kbt, vbt, ts):
    Bd, _, wb = kbt.shape
    for (W, d) in DILATED_PATTERNS:
        assert wb - (W // d) * d >= 0, "window buffer shorter than a pattern's reach"
    tables = [jnp.asarray(t) for t in _sample_tables(wb, ts)]
    row = pl.BlockSpec((ts, SW), lambda b: (b, 0))
    buf = pl.BlockSpec((None, SW, wb), lambda b: (b, 0, 0))
    return pl.pallas_call(
        functools.partial(_attn_sample_body, ts=ts),
        grid=(Bd,),
        in_specs=[row, row, row, buf, buf] + [_const_spec(t.shape) for t in tables],
        out_specs=row,
        out_shape=jax.ShapeDtypeStruct((Bd * ts, SW), F32),
        compiler_params=_params(1),
        name="attn_sample",
    )(qs, kn, vn, kbt, vbt, *tables)


def _ffn_body(og_ref, os_ref, x_ref, wo_ref, wfg_ref, wfu_ref, wfd_ref, gf_ref, gl_ref, y_ref,
              *scratch, os_slabs):
    if os_slabs:
        stage, = scratch
        for r in range(RES):
            for sl in range(SW // LANES):
                stage[sl, pl.ds(r, SUB, stride=RES), :] = (
                    os_ref[r, :, sl * LANES:(sl + 1) * LANES].astype(F32))
        osw = jnp.concatenate([stage[sl] for sl in range(SW // LANES)], axis=1).astype(BF16)
    else:
        osw = os_ref[...].astype(BF16)
    og = og_ref[...].astype(BF16)
    h = x_ref[...] + _dot(og, wo_ref[0:GV, :]) + _dot(osw, wo_ref[GV:GV + SW, :])
    hn = _rms(h, gf_ref[...]).astype(BF16)
    acc = jnp.zeros(h.shape, F32)
    for c in range(D_FF // FF_CHUNK):
        cs = slice(c * FF_CHUNK, (c + 1) * FF_CHUNK)
        gate = _dot(hn, wfg_ref[:, cs])
        up = _dot(hn, wfu_ref[:, cs])
        act = (gate * _sigmoid(gate) * up).astype(BF16)
        acc = acc + _dot(act, wfd_ref[cs, :])
    y_ref[...] = _rms(h + acc, gl_ref[...])


def _ffn_weights(w_out, g_ffn, w_fg, w_fu, w_fd, g_final):
    return (w_out.astype(BF16), w_fg.astype(BF16), w_fu.astype(BF16), w_fd.astype(BF16),
            g_ffn.reshape(1, D_MODEL), g_final.reshape(1, D_MODEL))


def _ffn_prompt(og, o16, x, wts):
    B, T, _ = x.shape
    per = TILE // ROWS
    wo, wfg, wfu, wfd, gf, gl = wts
    tok = lambda c: pl.BlockSpec((None, ROWS, c), lambda b, i: (b, i, 0))
    slab = pl.BlockSpec((None, None, RES, SUB, SW), lambda b, i: (b, i // per, 0, i % per, 0))
    return pl.pallas_call(
        functools.partial(_ffn_body, os_slabs=True),
        grid=(B, T // ROWS),
        in_specs=[tok(GV), slab, tok(D_MODEL),
                  _resident(wo.shape), _resident(wfg.shape), _resident(wfu.shape),
                  _resident(wfd.shape), _const_spec(gf.shape), _const_spec(gl.shape)],
        out_specs=tok(D_MODEL),
        out_shape=jax.ShapeDtypeStruct((B, T, D_MODEL), F32),
        scratch_shapes=[pltpu.VMEM((SW // LANES, ROWS, LANES), F32)],
        compiler_params=_params(2),
        name="ffn_prompt",
    )(og, o16, x, wo, wfg, wfu, wfd, gf, gl)


def _ffn_sample(og, osw, x, wts):
    N = x.shape[0]
    wo, wfg, wfu, wfd, gf, gl = wts
    spec = lambda c: pl.BlockSpec((ROWS, c), lambda i: (i, 0))
    return pl.pallas_call(
        functools.partial(_ffn_body, os_slabs=False),
        grid=(N // ROWS,),
        in_specs=[spec(GV), spec(SW), spec(D_MODEL),
                  _resident(wo.shape), _resident(wfg.shape), _resident(wfu.shape),
                  _resident(wfd.shape), _const_spec(gf.shape), _const_spec(gl.shape)],
        out_specs=spec(D_MODEL),
        out_shape=jax.ShapeDtypeStruct((N, D_MODEL), F32),
        compiler_params=_params(1),
        name="ffn_sample",
    )(og, osw, x, wo, wfg, wfu, wfd, gf, gl)


def _layer_prompt(x, w_in, w_gup, b_gate, g_mix, g_gla, w_out, g_ffn, w_fg, w_fu, w_fd, g_last):
    B, T, _ = x.shape
    assert T % TILE == 0 and T >= WIN_MAX and WIN_MAX % ROWS == 0
    w, wg, bg, gm = _pack_weights(w_in, w_gup, b_gate, g_mix)
    qg, kg, vg, r, la, q16, k16, v16, kt, vt, kk, vk = _inproj_prompt(x, w, wg, bg, gm)
    og, S = _gla_prompt(qg, kg, vg, la, r, g_gla)
    o16 = _attn_prompt(q16, k16, v16, kt, vt)
    y = _ffn_prompt(og, o16, x, _ffn_weights(w_out, g_ffn, w_fg, w_fu, w_fd, g_last))
    keep = lambda a: jnp.transpose(a.reshape(B, SWA_HEADS, SWA_HD, WIN_MAX), (0, 3, 1, 2))
    return y, S, keep(kk), keep(vk)


def _layer_sample(x, s0, kbuf, vbuf, w_in, w_gup, b_gate, g_mix, g_gla, w_out, g_ffn,
                  w_fg, w_fu, w_fd, g_last):
    Bd, Ts, _ = x.shape
    N = Bd * Ts
    assert CHUNK % Ts == 0 and N % ROWS == 0
    w, wg, bg, gm = _pack_weights(w_in, w_gup, b_gate, g_mix)
    qg, kg, vg, r, la, qs, ks, vs = _inproj_sample(x.reshape(N, D_MODEL), w, wg, bg, gm)
    og, S = _gla_sample(qg, kg, vg, la, r, g_gla, s0, Ts)
    wb = kbuf.shape[1]
    feat = lambda a: jnp.transpose(a, (0, 2, 3, 1)).reshape(Bd, SW, wb)
    osw = _attn_sample(qs, ks, vs, feat(kbuf), feat(vbuf), Ts)
    y = _ffn_sample(og, osw, x.reshape(N, D_MODEL),
                    _ffn_weights(w_out, g_ffn, w_fg, w_fu, w_fd, g_last))
    new = lambda a: a.reshape(Bd, Ts, SWA_HEADS, SWA_HD)
    return y.reshape(Bd, Ts, D_MODEL), S, new(ks), new(vs)


def kernel(x_prompt, x_sample, state_gla, cache_swa_k, cache_swa_v, w_in, w_gate_up, b_gate,
           g_mix_norm, g_gla_norm, w_out, g_ffn_norm, w_ffn_gate, w_ffn_up, w_ffn_down, g_final):
    depth = w_in.shape[0]
    assert depth == 1, "the final norm is fused into the single layer's FFN kernel"
    wl = (w_in[0], w_gate_up[0], b_gate[0], g_mix_norm[0], g_gla_norm[0], w_out[0],
          g_ffn_norm[0], w_ffn_gate[0], w_ffn_up[0], w_ffn_down[0], g_final)
    yp, sp, kp, vp = _layer_prompt(x_prompt, *wl)
    ys, ss, ksn, vsn = _layer_sample(x_sample, state_gla[0], cache_swa_k[0], cache_swa_v[0], *wl)
    return (yp, ys, sp[None], ss[None], kp[None], vp[None], ksn[None], vsn[None])
```

```python
import functools

import numpy as np
import jax
import jax.numpy as jnp
from jax import lax
from jax.experimental import pallas as pl
from jax.experimental.pallas import tpu as pltpu

F32 = jnp.float32
BF16 = jnp.bfloat16

D_MODEL = 1024
GLA_HEADS = 4
GLA_DK = 64
GLA_DV = 128
GLA_RANK = 16
GLA_TEMP = 16.0
SWA_HEADS = 8
SWA_HD = 64
DILATED_PATTERNS = ((128, 1), (512, 4), (2048, 16))
WIN_MAX = 2048
D_FF = 2816
EPS = 1e-6

GK = GLA_HEADS * GLA_DK
GV = GLA_HEADS * GLA_DV
SW = SWA_HEADS * SWA_HD
LANES = 128
TILE = 2048
RES = 16
BLK = 128
CHUNK = 64
ROWS = 512
SUB = ROWS // RES
NEG = -1e30
LOG2E = float(np.log2(np.e))
FF_CHUNK = 256
C_QG, C_KG, C_VG, C_R, C_QS, C_KS, C_VS, C_LR, C_END = 0, 256, 512, 1024, 1536, 2048, 2560, 3072, 3200
VMEM_LIMIT = 56 * 1024 * 1024


def _rms(x, g):
    ms = jnp.mean(x * x, axis=-1, keepdims=True)
    return x * lax.rsqrt(ms + EPS) * g


def _sigmoid(x):
    return 1.0 / (1.0 + jnp.exp(-x))


def _dot(a, b):
    return jnp.dot(a, b, preferred_element_type=F32)


def _dot_nt(a, b):
    return lax.dot_general(a, b, (((1,), (1,)), ((), ())), preferred_element_type=F32)


def _dot_tn(a, b):
    return lax.dot_general(a, b, (((0,), (0,)), ((), ())), preferred_element_type=F32)


def _const_spec(shape):
    nd = len(shape)
    return pl.BlockSpec(shape, lambda *_: (0,) * nd)


def _resident(shape):
    nd = len(shape)
    return pl.BlockSpec(shape, lambda *_: (0,) * nd, pipeline_mode=pl.Buffered(1))


def _params(n_axes):
    return pltpu.CompilerParams(dimension_semantics=("arbitrary",) * n_axes,
                                vmem_limit_bytes=VMEM_LIMIT)


def _inproj_body(x_ref, w_ref, wg_ref, bg_ref, gm_ref,
                 qg_ref, kg_ref, vg_ref, r_ref, la_ref, *rest, prompt, keep_first):
    xn = _rms(x_ref[...], gm_ref[...]).astype(BF16)

    def proj(lo, hi):
        return _dot(xn, w_ref[:, lo:hi])

    def gla_qk():
        qg_ref[...] = proj(C_QG, C_KG) * (GLA_DK ** -0.5)
        kg_ref[...] = proj(C_KG, C_VG)

    def gla_gate():
        lr = proj(C_LR, C_END)
        z = _dot(lr.astype(BF16), wg_ref[...]) + bg_ref[...]
        log_sig = -(jnp.maximum(-z, 0.0) + jnp.log1p(jnp.exp(-jnp.abs(z))))
        la_ref[...] = log_sig * (LOG2E / GLA_TEMP)

    def gla_rest():
        vg_ref[...] = proj(C_VG, C_R).astype(vg_ref.dtype)
        r_ref[...] = proj(C_R, C_QS)

    gla_gate()
    if not prompt:
        qs_ref, ks_ref, vs_ref = rest
        qs_ref[...] = proj(C_QS, C_KS) * (SWA_HD ** -0.5)
        ks_ref[...] = proj(C_KS, C_VS)
        vs_ref[...] = proj(C_VS, C_LR)
        gla_qk()
        gla_rest()
        return
    q16_ref, k16_ref, v16_ref, kt_ref, vt_ref, kk_ref, vk_ref, stage = rest

    def to_slabs(val, ref, buf):
        for sl in range(SW // LANES):
            stage[buf, sl] = val[:, sl * LANES:(sl + 1) * LANES]
        for r in range(RES):
            for sl in range(SW // LANES):
                ref[r, :, sl * LANES:(sl + 1) * LANES] = (
                    stage[buf, sl, pl.ds(r, SUB, stride=RES), :].astype(ref.dtype))

    k = proj(C_KS, C_VS)
    kt_ref[...] = k.astype(BF16)
    v = proj(C_VS, C_LR)
    to_slabs(k, k16_ref, 0)
    vt_ref[...] = v.astype(BF16)
    q = proj(C_QS, C_KS) * (SWA_HD ** -0.5 * LOG2E)
    to_slabs(v, v16_ref, 1)
    gla_qk()
    to_slabs(q, q16_ref, 2)
    gla_rest()

    @pl.when(pl.program_id(1) >= keep_first)
    def _():
        for sl in range(SW // LANES):
            kk_ref[sl * LANES:(sl + 1) * LANES, :] = stage[0, sl].T
            vk_ref[sl * LANES:(sl + 1) * LANES, :] = stage[1, sl].T


def _pack_weights(w_in, w_gup, b_gate, g_mix):
    sizes = (GK, GK, GV, GV, GLA_RANK, SW, SW, SW)
    offs = np.concatenate([[0], np.cumsum(sizes)])
    qg, kg, vg, r, lr, qs, ks, vs = [w_in[:, offs[i]:offs[i + 1]] for i in range(8)]
    lr = jnp.pad(lr, ((0, 0), (0, LANES - GLA_RANK)))
    w = jnp.concatenate([qg, kg, vg, r, qs, ks, vs, lr], axis=1).astype(BF16)
    wg = jnp.pad(w_gup, ((0, LANES - GLA_RANK), (0, 0))).astype(BF16)
    return w, wg, b_gate.reshape(1, GK), g_mix.reshape(1, D_MODEL)


def _inproj_prompt(x, w, wg, bg, gm):
    B, T, _ = x.shape
    nt, per = T // TILE, TILE // ROWS
    n = T // ROWS
    keep_first = n - WIN_MAX // ROWS
    tok = lambda c: pl.BlockSpec((None, ROWS, c), lambda b, i: (b, i, 0))
    slab = pl.BlockSpec((None, None, RES, SUB, SW), lambda b, i: (b, i // per, 0, i % per, 0))
    keep = pl.BlockSpec((None, SW, ROWS), lambda b, i: (b, 0, jnp.maximum(i - keep_first, 0)))
    tshape = lambda c, dt: jax.ShapeDtypeStruct((B, T, c), dt)
    sshape = lambda dt: jax.ShapeDtypeStruct((B, nt, RES, BLK, SW), dt)
    kshape = jax.ShapeDtypeStruct((B, SW, WIN_MAX), F32)
    return pl.pallas_call(
        functools.partial(_inproj_body, prompt=True, keep_first=keep_first),
        grid=(B, n),
        in_specs=[tok(D_MODEL), _resident(w.shape), _const_spec(wg.shape),
                  _const_spec(bg.shape), _const_spec(gm.shape)],
        out_specs=(tok(GK), tok(GK), tok(GV), tok(GV), tok(GK),
                   slab, slab, slab, tok(SW), tok(SW), keep, keep),
        out_shape=(tshape(GK, F32), tshape(GK, F32), tshape(GV, BF16), tshape(GV, F32),
                   tshape(GK, F32), sshape(F32), sshape(BF16), sshape(BF16),
                   tshape(SW, BF16), tshape(SW, BF16), kshape, kshape),
        scratch_shapes=[pltpu.VMEM((3, SW // LANES, ROWS, LANES), F32)],
        compiler_params=_params(2),
        name="inproj_prompt",
    )(x, w, wg, bg, gm)


def _inproj_sample(x, w, wg, bg, gm):
    N = x.shape[0]
    spec = lambda c: pl.BlockSpec((ROWS, c), lambda i: (i, 0))
    shp = lambda c: jax.ShapeDtypeStruct((N, c), F32)
    return pl.pallas_call(
        functools.partial(_inproj_body, prompt=False, keep_first=0),
        grid=(N // ROWS,),
        in_specs=[spec(D_MODEL), _resident(w.shape), _const_spec(wg.shape),
                  _const_spec(bg.shape), _const_spec(gm.shape)],
        out_specs=(spec(GK), spec(GK), spec(GV), spec(GV), spec(GK), spec(SW), spec(SW), spec(SW)),
        out_shape=(shp(GK), shp(GK), shp(GV), shp(GV), shp(GK), shp(SW), shp(SW), shp(SW)),
        compiler_params=_params(1),
        name="inproj_sample",
    )(x, w, wg, bg, gm)


def _gla_masks(C, seg):
    ti = lax.broadcasted_iota(jnp.int32, (C, C), 0)
    si = lax.broadcasted_iota(jnp.int32, (C, C), 1)
    masks = [ti == si]
    h, sh = 1, 0
    while h < seg:
        masks.append(jnp.logical_and(((ti ^ si) >> sh) == 1, si < ti))
        h, sh = 2 * h, sh + 1
    return masks


def _gla_chunk(q, k, v, la, r, g, seg, states, masks):
    C = q.shape[0]
    nseg = C // seg
    rows = lax.broadcasted_iota(jnp.int32, (C, GK), 0)
    rin = rows & (seg - 1)
    b_inc = la
    d = 1
    while d < seg:
        b_inc = b_inc + jnp.where(rin >= d, pltpu.roll(b_inc, d, 0), 0.0)
        d *= 2

    lane = lax.broadcasted_iota(jnp.int32, (C, LANES), 1)
    head_lo = lane < GLA_DK
    row_c = lax.broadcasted_iota(jnp.int32, (C, LANES), 0)

    def pair(a, p):
        return a[:, p * LANES:(p + 1) * LANES]

    def by_head(a):
        return jnp.concatenate([jnp.where(head_lo, a, 0.0), jnp.where(head_lo, 0.0, a)], axis=0)

    A = [jnp.zeros((C, C), F32) for _ in range(GLA_HEADS)]

    def add_level(qs, ks, mask):
        for p in range(2):
            a2 = _dot_nt(by_head(pair(qs, p)).astype(BF16), pair(ks, p).astype(BF16))
            A[2 * p] = jnp.where(mask, a2[:C], A[2 * p])
            A[2 * p + 1] = jnp.where(mask, a2[C:], A[2 * p + 1])

    def neg_abs(x):
        return lax.bitcast_convert_type(
            lax.bitcast_convert_type(x, jnp.uint32) | jnp.uint32(0x80000000), F32)

    SUBL = 8
    last8 = jnp.broadcast_to(b_inc.reshape(C // SUBL, SUBL, GK)[:, SUBL - 1:, :],
                             (C // SUBL, SUBL, GK))

    def group_end(h):
        n = C // (2 * h)
        picked = last8.reshape(n, 2 * h // SUBL, SUBL, GK)[:, h // SUBL - 1:h // SUBL]
        return jnp.broadcast_to(picked, (n, 2 * h // SUBL, SUBL, GK)).reshape(C, GK)

    add_level(q, k, masks[0])
    end = b_inc
    h, lvl = 1, 1
    while h < seg:
        if h >= SUBL:
            bound = group_end(h)
        else:
            upper = (rin & (2 * h - 1)) >= h
            bound = jnp.where(upper, pltpu.roll(end, h, 0), end)
            end = jnp.where(upper, end, pltpu.roll(end, C - h, 0))
        fac = jnp.exp2(neg_abs(b_inc - bound))
        add_level(q * fac, k * fac, masks[lvl])
        h, lvl = 2 * h, lvl + 1
    if seg >= SUBL:
        end = jnp.broadcast_to(last8.reshape(nseg, seg // SUBL, SUBL, GK)[:, seg // SUBL - 1:],
                               (nseg, seg // SUBL, SUBL, GK)).reshape(C, GK)
    q_int = q * jnp.exp2(b_inc)
    k_upd = k * jnp.exp2(end - b_inc)

    def seg_rows(j):
        return jnp.logical_and(row_c >= j * seg, row_c < (j + 1) * seg)

    outs = []
    for p in range(2):
        qp = pair(q_int, p)
        inter = jnp.zeros((2 * C, LANES), F32)
        for j in range(nseg):
            qj = qp if nseg == 1 else jnp.where(seg_rows(j), qp, 0.0)
            inter = inter + _dot(by_head(qj).astype(BF16), states[j][p].astype(BF16))
        for hp in range(2):
            hd = 2 * p + hp
            vh = v[:, hd * GLA_DV:(hd + 1) * GLA_DV]
            o = _dot(A[hd].astype(BF16), vh) + inter[hp * C:(hp + 1) * C]
            gh = g[:, hd * GLA_DV:(hd + 1) * GLA_DV]
            rh = r[:, hd * GLA_DV:(hd + 1) * GLA_DV]
            outs.append(_rms(o, gh) * (rh * _sigmoid(rh)))

    r128 = lax.broadcasted_iota(jnp.int32, (LANES, LANES), 0)
    c128 = lax.broadcasted_iota(jnp.int32, (LANES, LANES), 1)
    new_states = []
    for j in range(nseg):
        st = []
        for p in range(2):
            kp = pair(k_upd, p)
            if nseg > 1:
                kp = jnp.where(seg_rows(j), kp, 0.0)
            kp = kp.astype(BF16)
            t0 = _dot_tn(kp, v[:, (2 * p) * GLA_DV:(2 * p + 1) * GLA_DV])
            t1 = _dot_tn(kp, v[:, (2 * p + 1) * GLA_DV:(2 * p + 2) * GLA_DV])
            t = jnp.where(r128 < GLA_DK, t0, t1)
            e_row = jnp.exp2(pair(end, p)[j * seg:j * seg + 1, :])
            e_col = jnp.sum(jnp.where(r128 == c128, jnp.broadcast_to(e_row, (LANES, LANES)), 0.0),
                            axis=1, keepdims=True)
            st.append(e_col * states[j][p] + t)
        new_states.append(st)
    return jnp.concatenate(outs, axis=1), new_states


def _gla_prompt_body(q_ref, k_ref, v_ref, la_ref, r_ref, g_ref, o_ref, s_ref, s_scr, *, n_chunks):
    i = pl.program_id(1)

    @pl.when(i == 0)
    def _():
        s_scr[...] = jnp.zeros_like(s_scr)

    g = g_ref[...]
    masks = _gla_masks(CHUNK, CHUNK)

    def step(c, carry):
        sl = pl.ds(pl.multiple_of(c * CHUNK, CHUNK), CHUNK)
        o, st = _gla_chunk(q_ref[sl, :], k_ref[sl, :], v_ref[sl, :], la_ref[sl, :], r_ref[sl, :],
                           g, CHUNK, [[s_scr[0], s_scr[1]]], masks)
        o_ref[sl, :] = o.astype(o_ref.dtype)
        s_scr[0] = st[0][0]
        s_scr[1] = st[0][1]
        return carry

    lax.fori_loop(0, n_chunks, step, 0, unroll=True)

    @pl.when(i == pl.num_programs(1) - 1)
    def _():
        s_ref[...] = s_scr[...]


def _gla_prompt(qg, kg, vg, la, r, g_gla):
    B, T, _ = qg.shape
    spec = lambda c: pl.BlockSpec((None, ROWS, c), lambda b, i: (b, i, 0))
    og, s = pl.pallas_call(
        functools.partial(_gla_prompt_body, n_chunks=ROWS // CHUNK),
        grid=(B, T // ROWS),
        in_specs=[spec(GK), spec(GK), spec(GV), spec(GK), spec(GV), _const_spec((1, GV))],
        out_specs=(spec(GV), pl.BlockSpec((None, 2, LANES, LANES), lambda b, i: (b, 0, 0, 0))),
        out_shape=(jax.ShapeDtypeStruct((B, T, GV), BF16),
                   jax.ShapeDtypeStruct((B, 2, LANES, LANES), F32)),
        scratch_shapes=[pltpu.VMEM((2, LANES, LANES), F32)],
        compiler_params=_params(2),
        name="gla_prompt",
    )(qg, kg, vg, la, r, g_gla.reshape(1, GV))
    return og, s.reshape(B, GLA_HEADS, GLA_DK, GLA_DV)


def _gla_sample_body(q_ref, k_ref, v_ref, la_ref, r_ref, g_ref, s0_ref, o_ref, s_ref, *, seg):
    nseg = CHUNK // seg
    states = [[s0_ref[j, 0], s0_ref[j, 1]] for j in range(nseg)]
    o, st = _gla_chunk(q_ref[...], k_ref[...], v_ref[...].astype(BF16), la_ref[...], r_ref[...],
                       g_ref[...], seg, states, _gla_masks(CHUNK, seg))
    o_ref[...] = o
    for j in range(nseg):
        s_ref[j, 0] = st[j][0]
        s_ref[j, 1] = st[j][1]


def _gla_sample(qg, kg, vg, la, r, g_gla, s0, seg):
    N = qg.shape[0]
    nb = CHUNK // seg
    nseq = N // seg
    spec = lambda c: pl.BlockSpec((CHUNK, c), lambda i: (i, 0))
    sspec = pl.BlockSpec((nb, 2, LANES, LANES), lambda i: (i, 0, 0, 0))
    og, s = pl.pallas_call(
        functools.partial(_gla_sample_body, seg=seg),
        grid=(N // CHUNK,),
        in_specs=[spec(GK), spec(GK), spec(GV), spec(GK), spec(GV), _const_spec((1, GV)), sspec],
        out_specs=(spec(GV), sspec),
        out_shape=(jax.ShapeDtypeStruct((N, GV), F32),
                   jax.ShapeDtypeStruct((nseq, 2, LANES, LANES), F32)),
        compiler_params=_params(1),
        name="gla_sample",
    )(qg, kg, vg, la, r, g_gla.reshape(1, GV), s0.reshape(nseq, 2, LANES, LANES))
    return og, s.reshape(nseq, GLA_HEADS, GLA_DK, GLA_DV)


def _alibi_slopes():
    return np.exp2(-8.0 * (np.arange(SWA_HEADS, dtype=np.float64) + 1.0) / SWA_HEADS)


def _prompt_bias_tables():
    slopes = _alibi_slopes()
    idx = np.arange(BLK)
    q_nat = {16: idx, 4: 4 * (idx % 32) + idx // 32, 1: 16 * (idx % 8) + idx // 8}
    k_nat = {16: idx, 4: 4 * (idx % 32) + idx // 32, 1: idx}
    out = np.empty((3, 2, SWA_HEADS, BLK, 2 * BLK), np.float32)
    for pi, d in enumerate((16, 4, 1)):
        j = np.concatenate([k_nat[d], BLK + k_nat[d]])[None, :]
        stp = q_nat[d][:, None] + BLK - j
        band = (stp >= 0) & (stp <= BLK)
        for first in (0, 1):
            valid = band & ((j >= BLK) if first else True)
            bias = -LOG2E * slopes[:, None, None] * (stp * d).astype(np.float64)[None]
            out[pi, first] = np.where(valid[None], bias, NEG).astype(np.float32)
    return out


def _attend(q, kk, vv, bias_of, prev):
    lo = lax.broadcasted_iota(jnp.int32, (BLK, LANES), 1) < SWA_HD
    lo_k = lax.broadcasted_iota(jnp.int32, (2 * BLK, LANES), 1) < SWA_HD
    zero, one = jnp.zeros_like(q), jnp.ones_like(vv)
    q2 = jnp.concatenate([jnp.where(lo, q, zero), jnp.where(lo, zero, q)], axis=0)
    s2 = _dot_nt(q2, kk)
    m_new, res, alphas = [], [], []
    for hp in range(2):
        s = s2[hp * BLK:(hp + 1) * BLK] + bias_of(hp)
        mc = jnp.max(s, axis=1, keepdims=True)
        if prev is None:
            mn = jnp.broadcast_to(mc, (BLK, LANES))
        else:
            mn = jnp.maximum(prev[0][hp], mc)
            alphas.append(jnp.exp2(prev[0][hp] - mn))
        p = jnp.exp2(s - jnp.concatenate([mn, mn], axis=1)).astype(BF16)
        w = jnp.where(lo_k, vv, one) if hp == 0 else jnp.where(lo_k, one, vv)
        res.append(_dot(p, w))
        m_new.append(mn)
    acc = jnp.where(lo, res[0], res[1])
    l = jnp.where(lo, res[1], res[0])
    if prev is not None:
        acc = jnp.where(lo, alphas[0], alphas[1]) * prev[2] + acc
        l = jnp.where(lo, alphas[1], alphas[0]) * prev[1] + l
    return m_new, l, acc


def _attn_prompt_body(q_ref, kc_ref, vc_ref, kp_ref, vp_ref, ktc_ref, vtc_ref, ktp_ref, vtp_ref,
                      bias_ref, o_ref, m_scr, l_scr, a_scr, tk_scr, tv_scr):
    tile0 = (pl.program_id(2) == 0).astype(jnp.int32)
    tk_scr[0:BLK, :] = ktp_ref[...]
    tk_scr[BLK:, :] = ktc_ref[...]
    tv_scr[0:BLK, :] = vtp_ref[...]
    tv_scr[BLK:, :] = vtc_ref[...]

    for r in range(RES):
        kk = jnp.concatenate([kp_ref[r], kc_ref[r]], axis=0)
        vv = jnp.concatenate([vp_ref[r], vc_ref[r]], axis=0)
        m, l, a = _attend(q_ref[r].astype(BF16), kk, vv, lambda hp: bias_ref[0, tile0, hp], None)
        for hp in range(2):
            m_scr[hp, r] = m[hp]
        l_scr[r] = l
        a_scr[r] = a

    for r4 in range(4):
        for kb in range(4):
            rows = slice(32 * kb, 32 * kb + 32)

            def keys(cur, prev):
                if kb == 0:
                    before = [prev[4 * c + r4, 96:128, :] for c in range(4)]
                else:
                    before = [cur[4 * c + r4, 32 * kb - 32:32 * kb, :] for c in range(4)]
                return jnp.concatenate(before + [cur[4 * c + r4, rows, :] for c in range(4)], axis=0)

            cat = lambda ref, *i: jnp.concatenate([ref[(*i, 4 * c + r4, rows)] for c in range(4)], axis=0)
            first = tile0 if kb == 0 else 0
            prev = ([cat(m_scr, hp) for hp in range(2)], cat(l_scr), cat(a_scr))
            m, l, a = _attend(cat(q_ref).astype(BF16), keys(kc_ref, kp_ref), keys(vc_ref, vp_ref),
                              lambda hp: bias_ref[1, first, hp], prev)
            for c in range(4):
                cs = slice(32 * c, 32 * c + 32)
                for hp in range(2):
                    m_scr[hp, 4 * c + r4, rows, :] = m[hp][cs]
                l_scr[4 * c + r4, rows, :] = l[cs]
                a_scr[4 * c + r4, rows, :] = a[cs]

    for j in range(RES):
        rs = slice(j * 8, j * 8 + 8)
        ks = slice(j * BLK, j * BLK + 2 * BLK)
        cat = lambda ref, *idx: ref[(*idx, slice(None), rs, slice(None))].reshape(BLK, LANES)
        first = tile0 if j == 0 else 0
        prev = ([cat(m_scr, hp) for hp in range(2)], cat(l_scr), cat(a_scr))
        m, l, a = _attend(cat(q_ref).astype(BF16), tk_scr[ks, :], tv_scr[ks, :],
                          lambda hp: bias_ref[2, first, hp], prev)
        for hp in range(2):
            m_scr[hp, :, rs, :] = m[hp].reshape(RES, 8, LANES)
        l_scr[:, rs, :] = l.reshape(RES, 8, LANES)
        a_scr[:, rs, :] = a.reshape(RES, 8, LANES)

    for r in range(RES):
        o_ref[r] = (a_scr[r] / pltpu.roll(l_scr[r], SWA_HD, 1)).astype(o_ref.dtype)


def _attn_prompt(q16, k16, v16, kt, vt):
    B, nt = q16.shape[0], q16.shape[1]
    bias = jnp.asarray(_prompt_bias_tables())
    tile = (None, None, RES, BLK, LANES)
    cur = pl.BlockSpec(tile, lambda g, b, a: (b, a, 0, 0, g))
    prv = pl.BlockSpec(tile, lambda g, b, a: (b, jnp.maximum(a - 1, 0), 0, 0, g))
    tcur = pl.BlockSpec((None, TILE, LANES), lambda g, b, a: (b, a, g))
    tprv = pl.BlockSpec((None, BLK, LANES),
                        lambda g, b, a: (b, jnp.maximum(a * (TILE // BLK) - 1, 0), g))
    bias_spec = pl.BlockSpec((3, 2, 2, BLK, 2 * BLK), lambda g, b, a: (0, 0, g, 0, 0))
    return pl.pallas_call(
        _attn_prompt_body,
        grid=(SWA_HEADS // 2, B, nt),
        in_specs=[cur, cur, cur, prv, prv, tcur, tcur, tprv, tprv, bias_spec],
        out_specs=cur,
        out_shape=jax.ShapeDtypeStruct((B, nt, RES, BLK, SW), BF16),
        scratch_shapes=[pltpu.VMEM((2, RES, BLK, LANES), F32),
                        pltpu.VMEM((RES, BLK, LANES), F32),
                        pltpu.VMEM((RES, BLK, LANES), F32),
                        pltpu.VMEM((TILE + BLK, LANES), BF16),
                        pltpu.VMEM((TILE + BLK, LANES), BF16)],
        compiler_params=_params(3),
        name="attn_prompt",
    )(q16, k16, v16, k16, v16, kt, vt, kt, vt, bias)


def _sample_tables(wb, ts):
    slopes = _alibi_slopes()
    ncol = wb + LANES
    t = np.arange(ts)[:, None]
    c = np.arange(ncol)[None, :]
    dist = wb + t - c
    mult = np.zeros((ts, ncol), np.float64)
    for (W, d) in DILATED_PATTERNS:
        mult += (dist >= 0) & (dist % d == 0) & (dist // d <= W // d) & (c < wb + ts)
    bias = np.where(mult[None] > 0, -slopes[:, None, None] * dist[None].astype(np.float64), NEG)
    mult = np.broadcast_to(mult[None], bias.shape)
    shp = (SWA_HEADS * ts, ncol)
    bias = bias.reshape(shp).astype(np.float32)
    mult = mult.reshape(shp).astype(np.float32)
    return bias[:, :wb], mult[:, :wb], bias[:, wb:], mult[:, wb:]


def _attn_sample_body(q_ref, kn_ref, vn_ref, kt_ref, vt_ref, bb_ref, mb_ref, bn_ref, mn_ref, o_ref,
                      *, ts):
    nr = SWA_HEADS * ts
    q = q_ref[...]
    qt = jnp.concatenate([q] * SWA_HEADS, axis=0)
    rh = lax.broadcasted_iota(jnp.int32, (nr, SW), 0) // ts
    ch = lax.broadcasted_iota(jnp.int32, (nr, SW), 1) // SWA_HD
    same = rh == ch
    qbd = jnp.where(same, qt, 0.0).astype(BF16)
    zpad = jnp.zeros((LANES - ts, SW), F32)
    kn = jnp.concatenate([kn_ref[...], zpad], axis=0).astype(BF16)
    vn = jnp.concatenate([vn_ref[...], zpad], axis=0).astype(BF16)
    s_b = _dot(qbd, kt_ref[...].astype(BF16)) + bb_ref[...]
    s_n = _dot_nt(qbd, kn) + bn_ref[...]
    m = jnp.maximum(jnp.max(s_b, axis=1, keepdims=True), jnp.max(s_n, axis=1, keepdims=True))
    p_b = jnp.exp(s_b - m) * mb_ref[...]
    p_n = jnp.exp(s_n - m) * mn_ref[...]
    den = jnp.sum(p_b, axis=1, keepdims=True) + jnp.sum(p_n, axis=1, keepdims=True)
    num = _dot_nt(p_b.astype(BF16), vt_ref[...].astype(BF16)) + _dot(p_n.astype(BF16), vn)
    full = jnp.where(same, num / den, 0.0)
    o = full[0:ts]
    for h in range(1, SWA_HEADS):
        o = o + full[h * ts:(h + 1) * ts]
    o_ref[...] = o


def _attn_sample(qs, kn, vn, kbt, vbt, ts):
    Bd, _, wb = kbt.shape
    for (W, d) in DILATED_PATTERNS:
        assert wb - (W // d) * d >= 0, "window buffer shorter than a pattern's reach"
    tables = [jnp.asarray(t) for t in _sample_tables(wb, ts)]
    row = pl.BlockSpec((ts, SW), lambda b: (b, 0))
    buf = pl.BlockSpec((None, SW, wb), lambda b: (b, 0, 0))
    return pl.pallas_call(
        functools.partial(_attn_sample_body, ts=ts),
        grid=(Bd,),
        in_specs=[row, row, row, buf, buf] + [_const_spec(t.shape) for t in tables],
        out_specs=row,
        out_shape=jax.ShapeDtypeStruct((Bd * ts, SW), F32),
        compiler_params=_params(1),
        name="attn_sample",
    )(qs, kn, vn, kbt, vbt, *tables)


def _ffn_body(og_ref, os_ref, x_ref, wo_ref, wfg_ref, wfu_ref, wfd_ref, gf_ref, gl_ref, y_ref,
              *scratch, os_slabs):
    if os_slabs:
        stage, = scratch
        for r in range(RES):
            for sl in range(SW // LANES):
                stage[sl, pl.ds(r, SUB, stride=RES), :] = (
                    os_ref[r, :, sl * LANES:(sl + 1) * LANES].astype(F32))
        osw = jnp.concatenate([stage[sl] for sl in range(SW // LANES)], axis=1).astype(BF16)
    else:
        osw = os_ref[...].astype(BF16)
    og = og_ref[...].astype(BF16)
    h = x_ref[...] + _dot(og, wo_ref[0:GV, :]) + _dot(osw, wo_ref[GV:GV + SW, :])
    hn = _rms(h, gf_ref[...]).astype(BF16)
    acc = jnp.zeros(h.shape, F32)
    for c in range(D_FF // FF_CHUNK):
        cs = slice(c * FF_CHUNK, (c + 1) * FF_CHUNK)
        gate = _dot(hn, wfg_ref[:, cs])
        up = _dot(hn, wfu_ref[:, cs])
        act = (gate * _sigmoid(gate) * up).astype(BF16)
        acc = acc + _dot(act, wfd_ref[cs, :])
    y_ref[...] = _rms(h + acc, gl_ref[...])


def _ffn_weights(w_out, g_ffn, w_fg, w_fu, w_fd, g_final):
    return (w_out.astype(BF16), w_fg.astype(BF16), w_fu.astype(BF16), w_fd.astype(BF16),
            g_ffn.reshape(1, D_MODEL), g_final.reshape(1, D_MODEL))


def _ffn_prompt(og, o16, x, wts):
    B, T, _ = x.shape
    per = TILE // ROWS
    wo, wfg, wfu, wfd, gf, gl = wts
    tok = lambda c: pl.BlockSpec((None, ROWS, c), lambda b, i: (b, i, 0))
    slab = pl.BlockSpec((None, None, RES, SUB, SW), lambda b, i: (b, i // per, 0, i % per, 0))
    return pl.pallas_call(
        functools.partial(_ffn_body, os_slabs=True),
        grid=(B, T // ROWS),
        in_specs=[tok(GV), slab, tok(D_MODEL),
                  _resident(wo.shape), _resident(wfg.shape), _resident(wfu.shape),
                  _resident(wfd.shape), _const_spec(gf.shape), _const_spec(gl.shape)],
        out_specs=tok(D_MODEL),
        out_shape=jax.ShapeDtypeStruct((B, T, D_MODEL), F32),
        scratch_shapes=[pltpu.VMEM((SW // LANES, ROWS, LANES), F32)],
        compiler_params=_params(2),
        name="ffn_prompt",
    )(og, o16, x, wo, wfg, wfu, wfd, gf, gl)


def _ffn_sample(og, osw, x, wts):
    N = x.shape[0]
    wo, wfg, wfu, wfd, gf, gl = wts
    spec = lambda c: pl.BlockSpec((ROWS, c), lambda i: (i, 0))
    return pl.pallas_call(
        functools.partial(_ffn_body, os_slabs=False),
        grid=(N // ROWS,),
        in_specs=[spec(GV), spec(SW), spec(D_MODEL),
                  _resident(wo.shape), _resident(wfg.shape), _resident(wfu.shape),
                  _resident(wfd.shape), _const_spec(gf.shape), _const_spec(gl.shape)],
        out_specs=spec(D_MODEL),
        out_shape=jax.ShapeDtypeStruct((N, D_MODEL), F32),
        compiler_params=_params(1),
        name="ffn_sample",
    )(og, osw, x, wo, wfg, wfu, wfd, gf, gl)


def _layer_prompt(x, w_in, w_gup, b_gate, g_mix, g_gla, w_out, g_ffn, w_fg, w_fu, w_fd, g_last):
    B, T, _ = x.shape
    assert T % TILE == 0 and T >= WIN_MAX and WIN_MAX % ROWS == 0
    w, wg, bg, gm = _pack_weights(w_in, w_gup, b_gate, g_mix)
    qg, kg, vg, r, la, q16, k16, v16, kt, vt, kk, vk = _inproj_prompt(x, w, wg, bg, gm)
    og, S = _gla_prompt(qg, kg, vg, la, r, g_gla)
    o16 = _attn_prompt(q16, k16, v16, kt, vt)
    y = _ffn_prompt(og, o16, x, _ffn_weights(w_out, g_ffn, w_fg, w_fu, w_fd, g_last))
    keep = lambda a: jnp.transpose(a.reshape(B, SWA_HEADS, SWA_HD, WIN_MAX), (0, 3, 1, 2))
    return y, S, keep(kk), keep(vk)


def _layer_sample(x, s0, kbuf, vbuf, w_in, w_gup, b_gate, g_mix, g_gla, w_out, g_ffn,
                  w_fg, w_fu, w_fd, g_last):
    Bd, Ts, _ = x.shape
    N = Bd * Ts
    assert CHUNK % Ts == 0 and N % ROWS == 0
    w, wg, bg, gm = _pack_weights(w_in, w_gup, b_gate, g_mix)
    qg, kg, vg, r, la, qs, ks, vs = _inproj_sample(x.reshape(N, D_MODEL), w, wg, bg, gm)
    og, S = _gla_sample(qg, kg, vg, la, r, g_gla, s0, Ts)
    wb = kbuf.shape[1]
    feat = lambda a: jnp.transpose(a, (0, 2, 3, 1)).reshape(Bd, SW, wb)
    osw = _attn_sample(qs, ks, vs, feat(kbuf), feat(vbuf), Ts)
    y = _ffn_sample(og, osw, x.reshape(N, D_MODEL),
                    _ffn_weights(w_out, g_ffn, w_fg, w_fu, w_fd, g_last))
    new = lambda a: a.reshape(Bd, Ts, SWA_HEADS, SWA_HD)
    return y.reshape(Bd, Ts, D_MODEL), S, new(ks), new(vs)


def kernel(x_prompt, x_sample, state_gla, cache_swa_k, cache_swa_v, w_in, w_gate_up, b_gate,
           g_mix_norm, g_gla_norm, w_out, g_ffn_norm, w_ffn_gate, w_ffn_up, w_ffn_down, g_final):
    depth = w_in.shape[0]
    assert depth == 1, "the final norm is fused into the single layer's FFN kernel"
    wl = (w_in[0], w_gate_up[0], b_gate[0], g_mix_norm[0], g_gla_norm[0], w_out[0],
          g_ffn_norm[0], w_ffn_gate[0], w_ffn_up[0], w_ffn_down[0], g_final)
    yp, sp, kp, vp = _layer_prompt(x_prompt, *wl)
    ys, ss, ksn, vsn = _layer_sample(x_sample, state_gla[0], cache_swa_k[0], cache_swa_v[0], *wl)
    return (yp, ys, sp[None], ss[None], kp[None], vp[None], ksn[None], vsn[None])
```

```python
import functools

import numpy as np
import jax
import jax.numpy as jnp
from jax import lax
from jax.experimental import pallas as pl
from jax.experimental.pallas import tpu as pltpu

F32 = jnp.float32
BF16 = jnp.bfloat16

D_MODEL = 1024
GLA_HEADS = 4
GLA_DK = 64
GLA_DV = 128
GLA_RANK = 16
GLA_TEMP = 16.0
SWA_HEADS = 8
SWA_HD = 64
DILATED_PATTERNS = ((128, 1), (512, 4), (2048, 16))
WIN_MAX = 2048
D_FF = 2816
EPS = 1e-6

GK = GLA_HEADS * GLA_DK
GV = GLA_HEADS * GLA_DV
SW = SWA_HEADS * SWA_HD
LANES = 128
TILE = 2048
RES = 16
BLK = 128
CHUNK = 64
ROWS = 512
SUB = ROWS // RES
NEG = -1e30
LOG2E = float(np.log2(np.e))
FF_CHUNK = 256
C_QG, C_KG, C_VG, C_R, C_QS, C_KS, C_VS, C_LR, C_END = 0, 256, 512, 1024, 1536, 2048, 2560, 3072, 3200
VMEM_LIMIT = 56 * 1024 * 1024


def _rms(x, g):
    ms = jnp.mean(x * x, axis=-1, keepdims=True)
    return x * lax.rsqrt(ms + EPS) * g


def _sigmoid(x):
    return 1.0 / (1.0 + jnp.exp(-x))


def _dot(a, b):
    return jnp.dot(a, b, preferred_element_type=F32)


def _dot_nt(a, b):
    return lax.dot_general(a, b, (((1,), (1,)), ((), ())), preferred_element_type=F32)


def _dot_tn(a, b):
    return lax.dot_general(a, b, (((0,), (0,)), ((), ())), preferred_element_type=F32)


def _const_spec(shape):
    nd = len(shape)
    return pl.BlockSpec(shape, lambda *_: (0,) * nd)


def _resident(shape):
    nd = len(shape)
    return pl.BlockSpec(shape, lambda *_: (0,) * nd, pipeline_mode=pl.Buffered(1))


def _params(n_axes):
    return pltpu.CompilerParams(dimension_semantics=("arbitrary",) * n_axes,
                                vmem_limit_bytes=VMEM_LIMIT)


def _inproj_body(x_ref, w_ref, wg_ref, bg_ref, gm_ref,
                 qg_ref, kg_ref, vg_ref, r_ref, la_ref, *rest, prompt, keep_first):
    xn = _rms(x_ref[...], gm_ref[...]).astype(BF16)

    def proj(lo, hi):
        return _dot(xn, w_ref[:, lo:hi])

    def gla_qk():
        qg_ref[...] = proj(C_QG, C_KG) * (GLA_DK ** -0.5)
        kg_ref[...] = proj(C_KG, C_VG)

    def gla_gate():
        lr = proj(C_LR, C_END)
        z = _dot(lr.astype(BF16), wg_ref[...]) + bg_ref[...]
        log_sig = -(jnp.maximum(-z, 0.0) + jnp.log1p(jnp.exp(-jnp.abs(z))))
        la_ref[...] = log_sig * (LOG2E / GLA_TEMP)

    def gla_rest():
        vg_ref[...] = proj(C_VG, C_R).astype(vg_ref.dtype)
        r_ref[...] = proj(C_R, C_QS)

    gla_gate()
    if not prompt:
        qs_ref, ks_ref, vs_ref = rest
        qs_ref[...] = proj(C_QS, C_KS) * (SWA_HD ** -0.5)
        ks_ref[...] = proj(C_KS, C_VS)
        vs_ref[...] = proj(C_VS, C_LR)
        gla_qk()
        gla_rest()
        return
    q16_ref, k16_ref, v16_ref, kt_ref, vt_ref, kk_ref, vk_ref, stage = rest

    def to_slabs(val, ref, buf):
        for sl in range(SW // LANES):
            stage[buf, sl] = val[:, sl * LANES:(sl + 1) * LANES]
        for r in range(RES):
            for sl in range(SW // LANES):
                ref[r, :, sl * LANES:(sl + 1) * LANES] = (
                    stage[buf, sl, pl.ds(r, SUB, stride=RES), :].astype(ref.dtype))

    k = proj(C_KS, C_VS)
    kt_ref[...] = k.astype(BF16)
    v = proj(C_VS, C_LR)
    to_slabs(k, k16_ref, 0)
    vt_ref[...] = v.astype(BF16)
    q = proj(C_QS, C_KS) * (SWA_HD ** -0.5 * LOG2E)
    to_slabs(v, v16_ref, 1)
    gla_qk()
    to_slabs(q, q16_ref, 2)
    gla_rest()

    @pl.when(pl.program_id(1) >= keep_first)
    def _():
        for sl in range(SW // LANES):
            kk_ref[sl * LANES:(sl + 1) * LANES, :] = stage[0, sl].T
            vk_ref[sl * LANES:(sl + 1) * LANES, :] = stage[1, sl].T


def _pack_weights(w_in, w_gup, b_gate, g_mix):
    sizes = (GK, GK, GV, GV, GLA_RANK, SW, SW, SW)
    offs = np.concatenate([[0], np.cumsum(sizes)])
    qg, kg, vg, r, lr, qs, ks, vs = [w_in[:, offs[i]:offs[i + 1]] for i in range(8)]
    lr = jnp.pad(lr, ((0, 0), (0, LANES - GLA_RANK)))
    w = jnp.concatenate([qg, kg, vg, r, qs, ks, vs, lr], axis=1).astype(BF16)
    wg = jnp.pad(w_gup, ((0, LANES - GLA_RANK), (0, 0))).astype(BF16)
    return w, wg, b_gate.reshape(1, GK), g_mix.reshape(1, D_MODEL)


def _inproj_prompt(x, w, wg, bg, gm):
    B, T, _ = x.shape
    nt, per = T // TILE, TILE // ROWS
    n = T // ROWS
    keep_first = n - WIN_MAX // ROWS
    tok = lambda c: pl.BlockSpec((None, ROWS, c), lambda b, i: (b, i, 0))
    slab = pl.BlockSpec((None, None, RES, SUB, SW), lambda b, i: (b, i // per, 0, i % per, 0))
    keep = pl.BlockSpec((None, SW, ROWS), lambda b, i: (b, 0, jnp.maximum(i - keep_first, 0)))
    tshape = lambda c, dt: jax.ShapeDtypeStruct((B, T, c), dt)
    sshape = lambda dt: jax.ShapeDtypeStruct((B, nt, RES, BLK, SW), dt)
    kshape = jax.ShapeDtypeStruct((B, SW, WIN_MAX), F32)
    return pl.pallas_call(
        functools.partial(_inproj_body, prompt=True, keep_first=keep_first),
        grid=(B, n),
        in_specs=[tok(D_MODEL), _resident(w.shape), _const_spec(wg.shape),
                  _const_spec(bg.shape), _const_spec(gm.shape)],
        out_specs=(tok(GK), tok(GK), tok(GV), tok(GV), tok(GK),
                   slab, slab, slab, tok(SW), tok(SW), keep, keep),
        out_shape=(tshape(GK, F32), tshape(GK, F32), tshape(GV, BF16), tshape(GV, F32),
                   tshape(GK, F32), sshape(F32), sshape(BF16), sshape(BF16),
                   tshape(SW, BF16), tshape(SW, BF16), kshape, kshape),
        scratch_shapes=[pltpu.VMEM((3, SW // LANES, ROWS, LANES), F32)],
        compiler_params=_params(2),
        name="inproj_prompt",
    )(x, w, wg, bg, gm)


def _inproj_sample(x, w, wg, bg, gm):
    N = x.shape[0]
    spec = lambda c: pl.BlockSpec((ROWS, c), lambda i: (i, 0))
    shp = lambda c: jax.ShapeDtypeStruct((N, c), F32)
    return pl.pallas_call(
        functools.partial(_inproj_body, prompt=False, keep_first=0),
        grid=(N // ROWS,),
        in_specs=[spec(D_MODEL), _resident(w.shape), _const_spec(wg.shape),
                  _const_spec(bg.shape), _const_spec(gm.shape)],
        out_specs=(spec(GK), spec(GK), spec(GV), spec(GV), spec(GK), spec(SW), spec(SW), spec(SW)),
        out_shape=(shp(GK), shp(GK), shp(GV), shp(GV), shp(GK), shp(SW), shp(SW), shp(SW)),
        compiler_params=_params(1),
        name="inproj_sample",
    )(x, w, wg, bg, gm)


def _gla_masks(C, seg):
    ti = lax.broadcasted_iota(jnp.int32, (C, C), 0)
    si = lax.broadcasted_iota(jnp.int32, (C, C), 1)
    rows = lax.broadcasted_iota(jnp.int32, (C, GK), 0)
    masks, signs = [ti == si], [None]
    h, sh = 1, 0
    while h < seg:
        masks.append(jnp.logical_and(((ti ^ si) >> sh) == 1, si < ti))
        signs.append(jnp.where((rows & (2 * h - 1)) >= h, 1.0, -1.0).astype(F32))
        h, sh = 2 * h, sh + 1
    return masks, signs


def _gla_chunk(q, k, v, la, r, g, seg, states, consts):
    C = q.shape[0]
    nseg = C // seg
    rows = lax.broadcasted_iota(jnp.int32, (C, GK), 0)
    rin = rows & (seg - 1)
    b_inc = la
    d = 1
    while d < seg:
        b_inc = b_inc + jnp.where(rin >= d, pltpu.roll(b_inc, d, 0), 0.0)
        d *= 2

    lane = lax.broadcasted_iota(jnp.int32, (C, LANES), 1)
    head_lo = lane < GLA_DK
    row_c = lax.broadcasted_iota(jnp.int32, (C, LANES), 0)

    def pair(a, p):
        return a[:, p * LANES:(p + 1) * LANES]

    def by_head(a):
        return jnp.concatenate([jnp.where(head_lo, a, 0.0), jnp.where(head_lo, 0.0, a)], axis=0)

    A = [jnp.zeros((C, C), F32) for _ in range(GLA_HEADS)]

    def add_level(qs, ks, mask):
        for p in range(2):
            a2 = _dot_nt(by_head(pair(qs, p)).astype(BF16), pair(ks, p).astype(BF16))
            A[2 * p] = jnp.where(mask, a2[:C], A[2 * p])
            A[2 * p + 1] = jnp.where(mask, a2[C:], A[2 * p + 1])

    masks, signs = consts
    SUBL = 8
    last8 = jnp.broadcast_to(b_inc.reshape(C // SUBL, SUBL, GK)[:, SUBL - 1:, :],
                             (C // SUBL, SUBL, GK))

    def group_end(h):
        n = C // (2 * h)
        picked = last8.reshape(n, 2 * h // SUBL, SUBL, GK)[:, h // SUBL - 1:h // SUBL]
        return jnp.broadcast_to(picked, (n, 2 * h // SUBL, SUBL, GK)).reshape(C, GK)

    add_level(q, k, masks[0])
    end = b_inc
    h, lvl = 1, 1
    while h < seg:
        if h >= SUBL:
            bound = group_end(h)
        else:
            upper = (rin & (2 * h - 1)) >= h
            bound = jnp.where(upper, pltpu.roll(end, h, 0), end)
            end = jnp.where(upper, end, pltpu.roll(end, C - h, 0))
        fac = jnp.exp2((b_inc - bound) * signs[lvl])
        add_level(q * fac, k * fac, masks[lvl])
        h, lvl = 2 * h, lvl + 1
    if seg >= SUBL:
        end = jnp.broadcast_to(last8.reshape(nseg, seg // SUBL, SUBL, GK)[:, seg // SUBL - 1:],
                               (nseg, seg // SUBL, SUBL, GK)).reshape(C, GK)
    q_int = q * jnp.exp2(b_inc)
    k_upd = k * jnp.exp2(end - b_inc)

    def seg_rows(j):
        return jnp.logical_and(row_c >= j * seg, row_c < (j + 1) * seg)

    outs = []
    for p in range(2):
        qp = pair(q_int, p)
        inter = jnp.zeros((2 * C, LANES), F32)
        for j in range(nseg):
            qj = qp if nseg == 1 else jnp.where(seg_rows(j), qp, 0.0)
            inter = inter + _dot(by_head(qj).astype(BF16), states[j][p].astype(BF16))
        for hp in range(2):
            hd = 2 * p + hp
            vh = v[:, hd * GLA_DV:(hd + 1) * GLA_DV]
            o = _dot(A[hd].astype(BF16), vh) + inter[hp * C:(hp + 1) * C]
            gh = g[:, hd * GLA_DV:(hd + 1) * GLA_DV]
            rh = r[:, hd * GLA_DV:(hd + 1) * GLA_DV]
            outs.append(_rms(o, gh) * (rh * _sigmoid(rh)))

    r128 = lax.broadcasted_iota(jnp.int32, (LANES, LANES), 0)
    c128 = lax.broadcasted_iota(jnp.int32, (LANES, LANES), 1)
    new_states = []
    for j in range(nseg):
        st = []
        for p in range(2):
            kp = pair(k_upd, p)
            if nseg > 1:
                kp = jnp.where(seg_rows(j), kp, 0.0)
            kp = kp.astype(BF16)
            t0 = _dot_tn(kp, v[:, (2 * p) * GLA_DV:(2 * p + 1) * GLA_DV])
            t1 = _dot_tn(kp, v[:, (2 * p + 1) * GLA_DV:(2 * p + 2) * GLA_DV])
            t = jnp.where(r128 < GLA_DK, t0, t1)
            e_row = jnp.exp2(pair(end, p)[j * seg:j * seg + 1, :])
            e_col = jnp.sum(jnp.where(r128 == c128, jnp.broadcast_to(e_row, (LANES, LANES)), 0.0),
                            axis=1, keepdims=True)
            st.append(e_col * states[j][p] + t)
        new_states.append(st)
    return jnp.concatenate(outs, axis=1), new_states


N_RIDER_IN = 9


def _gla_prompt_body(q_ref, k_ref, v_ref, la_ref, r_ref, g_ref, *rest, n_chunks, rider_ts):
    rider_in, (o_ref, s_ref, so_ref, s_scr) = rest[:N_RIDER_IN], rest[N_RIDER_IN:]
    i = pl.program_id(1)

    @pl.when(i == 0)
    def _():
        s_scr[...] = jnp.zeros_like(s_scr)

    _attn_sample_body(*rider_in, so_ref, ts=rider_ts)
    g = g_ref[...]
    masks = _gla_masks(CHUNK, CHUNK)

    def step(c, carry):
        sl = pl.ds(pl.multiple_of(c * CHUNK, CHUNK), CHUNK)
        o, st = _gla_chunk(q_ref[sl, :], k_ref[sl, :], v_ref[sl, :], la_ref[sl, :], r_ref[sl, :],
                           g, CHUNK, [[s_scr[0], s_scr[1]]], masks)
        o_ref[sl, :] = o.astype(o_ref.dtype)
        s_scr[0] = st[0][0]
        s_scr[1] = st[0][1]
        return carry

    lax.fori_loop(0, n_chunks, step, 0, unroll=True)

    @pl.when(i == pl.num_programs(1) - 1)
    def _():
        s_ref[...] = s_scr[...]


def _gla_prompt(qg, kg, vg, la, r, g_gla, sample):
    B, T, _ = qg.shape
    n = T // ROWS
    assert sample[3].shape[0] == B * n, "one sample sequence per prompt GLA grid step"
    spec = lambda c: pl.BlockSpec((None, ROWS, c), lambda b, i: (b, i, 0))
    r_ops, r_specs, r_out_spec, r_out_shape = _attn_sample_operands(*sample, lambda b, i: b * n + i)
    assert len(r_ops) == N_RIDER_IN
    og, s, oss = pl.pallas_call(
        functools.partial(_gla_prompt_body, n_chunks=ROWS // CHUNK, rider_ts=sample[5]),
        grid=(B, n),
        in_specs=[spec(GK), spec(GK), spec(GV), spec(GK), spec(GV), _const_spec((1, GV))] + r_specs,
        out_specs=(spec(GV), pl.BlockSpec((None, 2, LANES, LANES), lambda b, i: (b, 0, 0, 0)),
                   r_out_spec),
        out_shape=(jax.ShapeDtypeStruct((B, T, GV), BF16),
                   jax.ShapeDtypeStruct((B, 2, LANES, LANES), F32), r_out_shape),
        scratch_shapes=[pltpu.VMEM((2, LANES, LANES), F32)],
        compiler_params=_params(2),
        name="gla_prompt",
    )(qg, kg, vg, la, r, g_gla.reshape(1, GV), *r_ops)
    return og, s.reshape(B, GLA_HEADS, GLA_DK, GLA_DV), oss


def _gla_sample_body(q_ref, k_ref, v_ref, la_ref, r_ref, g_ref, s0_ref, o_ref, s_ref, *, seg):
    nseg = CHUNK // seg
    states = [[s0_ref[j, 0], s0_ref[j, 1]] for j in range(nseg)]
    o, st = _gla_chunk(q_ref[...], k_ref[...], v_ref[...].astype(BF16), la_ref[...], r_ref[...],
                       g_ref[...], seg, states, _gla_masks(CHUNK, seg))
    o_ref[...] = o
    for j in range(nseg):
        s_ref[j, 0] = st[j][0]
        s_ref[j, 1] = st[j][1]


def _gla_sample(qg, kg, vg, la, r, g_gla, s0, seg):
    N = qg.shape[0]
    nb = CHUNK // seg
    nseq = N // seg
    spec = lambda c: pl.BlockSpec((CHUNK, c), lambda i: (i, 0))
    sspec = pl.BlockSpec((nb, 2, LANES, LANES), lambda i: (i, 0, 0, 0))
    og, s = pl.pallas_call(
        functools.partial(_gla_sample_body, seg=seg),
        grid=(N // CHUNK,),
        in_specs=[spec(GK), spec(GK), spec(GV), spec(GK), spec(GV), _const_spec((1, GV)), sspec],
        out_specs=(spec(GV), sspec),
        out_shape=(jax.ShapeDtypeStruct((N, GV), F32),
                   jax.ShapeDtypeStruct((nseq, 2, LANES, LANES), F32)),
        compiler_params=_params(1),
        name="gla_sample",
    )(qg, kg, vg, la, r, g_gla.reshape(1, GV), s0.reshape(nseq, 2, LANES, LANES))
    return og, s.reshape(nseq, GLA_HEADS, GLA_DK, GLA_DV)


def _alibi_slopes():
    return np.exp2(-8.0 * (np.arange(SWA_HEADS, dtype=np.float64) + 1.0) / SWA_HEADS)


def _prompt_bias_tables():
    slopes = _alibi_slopes()
    idx = np.arange(BLK)
    q_nat = {16: idx, 4: 4 * (idx % 32) + idx // 32, 1: 16 * (idx % 8) + idx // 8}
    k_nat = {16: idx, 4: 4 * (idx % 32) + idx // 32, 1: idx}
    out = np.empty((3, 2, SWA_HEADS, BLK, 2 * BLK), np.float32)
    for pi, d in enumerate((16, 4, 1)):
        j = np.concatenate([k_nat[d], BLK + k_nat[d]])[None, :]
        stp = q_nat[d][:, None] + BLK - j
        band = (stp >= 0) & (stp <= BLK)
        for first in (0, 1):
            valid = band & ((j >= BLK) if first else True)
            bias = -LOG2E * slopes[:, None, None] * (stp * d).astype(np.float64)[None]
            out[pi, first] = np.where(valid[None], bias, NEG).astype(np.float32)
    return out


def _attend(q, kk, vv, bias_of, prev):
    lo = lax.broadcasted_iota(jnp.int32, (BLK, LANES), 1) < SWA_HD
    lo_k = lax.broadcasted_iota(jnp.int32, (2 * BLK, LANES), 1) < SWA_HD
    zero, one = jnp.zeros_like(q), jnp.ones_like(vv)
    q2 = jnp.concatenate([jnp.where(lo, q, zero), jnp.where(lo, zero, q)], axis=0)
    s2 = _dot_nt(q2, kk)
    m_new, res, alphas = [], [], []
    for hp in range(2):
        s = s2[hp * BLK:(hp + 1) * BLK] + bias_of(hp)
        mc = jnp.max(s, axis=1, keepdims=True)
        if prev is None:
            mn = jnp.broadcast_to(mc, (BLK, LANES))
        else:
            mn = jnp.maximum(prev[0][hp], mc)
            alphas.append(jnp.exp2(prev[0][hp] - mn))
        p = jnp.exp2(s - jnp.concatenate([mn, mn], axis=1)).astype(BF16)
        w = jnp.where(lo_k, vv, one) if hp == 0 else jnp.where(lo_k, one, vv)
        res.append(_dot(p, w))
        m_new.append(mn)
    acc = jnp.where(lo, res[0], res[1])
    l = jnp.where(lo, res[1], res[0])
    if prev is not None:
        acc = jnp.where(lo, alphas[0], alphas[1]) * prev[2] + acc
        l = jnp.where(lo, alphas[1], alphas[0]) * prev[1] + l
    return m_new, l, acc


def _attn_prompt_body(q_ref, kc_ref, vc_ref, kp_ref, vp_ref, ktc_ref, vtc_ref, ktp_ref, vtp_ref,
                      bias_ref, o_ref, m_scr, l_scr, a_scr, tk_scr, tv_scr):
    tile0 = (pl.program_id(2) == 0).astype(jnp.int32)
    tk_scr[0:BLK, :] = ktp_ref[...]
    tk_scr[BLK:, :] = ktc_ref[...]
    tv_scr[0:BLK, :] = vtp_ref[...]
    tv_scr[BLK:, :] = vtc_ref[...]

    for r in range(RES):
        kk = jnp.concatenate([kp_ref[r], kc_ref[r]], axis=0)
        vv = jnp.concatenate([vp_ref[r], vc_ref[r]], axis=0)
        m, l, a = _attend(q_ref[r].astype(BF16), kk, vv, lambda hp: bias_ref[0, tile0, hp], None)
        for hp in range(2):
            m_scr[hp, r] = m[hp]
        l_scr[r] = l
        a_scr[r] = a

    for r4 in range(4):
        for kb in range(4):
            rows = slice(32 * kb, 32 * kb + 32)

            def keys(cur, prev):
                if kb == 0:
                    before = [prev[4 * c + r4, 96:128, :] for c in range(4)]
                else:
                    before = [cur[4 * c + r4, 32 * kb - 32:32 * kb, :] for c in range(4)]
                return jnp.concatenate(before + [cur[4 * c + r4, rows, :] for c in range(4)], axis=0)

            cat = lambda ref, *i: jnp.concatenate([ref[(*i, 4 * c + r4, rows)] for c in range(4)], axis=0)
            first = tile0 if kb == 0 else 0
            prev = ([cat(m_scr, hp) for hp in range(2)], cat(l_scr), cat(a_scr))
            m, l, a = _attend(cat(q_ref).astype(BF16), keys(kc_ref, kp_ref), keys(vc_ref, vp_ref),
                              lambda hp: bias_ref[1, first, hp], prev)
            for c in range(4):
                cs = slice(32 * c, 32 * c + 32)
                for hp in range(2):
                    m_scr[hp, 4 * c + r4, rows, :] = m[hp][cs]
                l_scr[4 * c + r4, rows, :] = l[cs]
                a_scr[4 * c + r4, rows, :] = a[cs]

    for j in range(RES):
        rs = slice(j * 8, j * 8 + 8)
        ks = slice(j * BLK, j * BLK + 2 * BLK)
        cat = lambda ref, *idx: ref[(*idx, slice(None), rs, slice(None))].reshape(BLK, LANES)
        first = tile0 if j == 0 else 0
        prev = ([cat(m_scr, hp) for hp in range(2)], cat(l_scr), cat(a_scr))
        m, l, a = _attend(cat(q_ref).astype(BF16), tk_scr[ks, :], tv_scr[ks, :],
                          lambda hp: bias_ref[2, first, hp], prev)
        for hp in range(2):
            m_scr[hp, :, rs, :] = m[hp].reshape(RES, 8, LANES)
        l_scr[:, rs, :] = l.reshape(RES, 8, LANES)
        a_scr[:, rs, :] = a.reshape(RES, 8, LANES)

    for r in range(RES):
        o_ref[r] = (a_scr[r] / pltpu.roll(l_scr[r], SWA_HD, 1)).astype(o_ref.dtype)


def _attn_prompt(q16, k16, v16, kt, vt):
    B, nt = q16.shape[0], q16.shape[1]
    bias = jnp.asarray(_prompt_bias_tables())
    tile = (None, None, RES, BLK, LANES)
    cur = pl.BlockSpec(tile, lambda g, b, a: (b, a, 0, 0, g))
    prv = pl.BlockSpec(tile, lambda g, b, a: (b, jnp.maximum(a - 1, 0), 0, 0, g))
    tcur = pl.BlockSpec((None, TILE, LANES), lambda g, b, a: (b, a, g))
    tprv = pl.BlockSpec((None, BLK, LANES),
                        lambda g, b, a: (b, jnp.maximum(a * (TILE // BLK) - 1, 0), g))
    bias_spec = pl.BlockSpec((3, 2, 2, BLK, 2 * BLK), lambda g, b, a: (0, 0, g, 0, 0))
    return pl.pallas_call(
        _attn_prompt_body,
        grid=(SWA_HEADS // 2, B, nt),
        in_specs=[cur, cur, cur, prv, prv, tcur, tcur, tprv, tprv, bias_spec],
        out_specs=cur,
        out_shape=jax.ShapeDtypeStruct((B, nt, RES, BLK, SW), BF16),
        scratch_shapes=[pltpu.VMEM((2, RES, BLK, LANES), F32),
                        pltpu.VMEM((RES, BLK, LANES), F32),
                        pltpu.VMEM((RES, BLK, LANES), F32),
                        pltpu.VMEM((TILE + BLK, LANES), BF16),
                        pltpu.VMEM((TILE + BLK, LANES), BF16)],
        compiler_params=_params(3),
        name="attn_prompt",
    )(q16, k16, v16, k16, v16, kt, vt, kt, vt, bias)


def _sample_tables(wb, ts):
    slopes = _alibi_slopes()
    ncol = wb + LANES
    t = np.arange(ts)[:, None]
    c = np.arange(ncol)[None, :]
    dist = wb + t - c
    mult = np.zeros((ts, ncol), np.float64)
    for (W, d) in DILATED_PATTERNS:
        mult += (dist >= 0) & (dist % d == 0) & (dist // d <= W // d) & (c < wb + ts)
    bias = np.where(mult[None] > 0, -slopes[:, None, None] * dist[None].astype(np.float64), NEG)
    mult = np.broadcast_to(mult[None], bias.shape)
    shp = (SWA_HEADS * ts, ncol)
    bias = bias.reshape(shp).astype(np.float32)
    mult = mult.reshape(shp).astype(np.float32)
    return bias[:, :wb], mult[:, :wb], bias[:, wb:], mult[:, wb:]


def _attn_sample_body(q_ref, kn_ref, vn_ref, kt_ref, vt_ref, bb_ref, mb_ref, bn_ref, mn_ref, o_ref,
                      *, ts):
    nr = SWA_HEADS * ts
    q = q_ref[...]
    qt = jnp.concatenate([q] * SWA_HEADS, axis=0)
    rh = lax.broadcasted_iota(jnp.int32, (nr, SW), 0) // ts
    ch = lax.broadcasted_iota(jnp.int32, (nr, SW), 1) // SWA_HD
    same = rh == ch
    qbd = jnp.where(same, qt, 0.0).astype(BF16)
    zpad = jnp.zeros((LANES - ts, SW), F32)
    kn = jnp.concatenate([kn_ref[...], zpad], axis=0).astype(BF16)
    vn = jnp.concatenate([vn_ref[...], zpad], axis=0).astype(BF16)
    s_b = _dot(qbd, kt_ref[...].astype(BF16)) + bb_ref[...]
    s_n = _dot_nt(qbd, kn) + bn_ref[...]
    m = jnp.maximum(jnp.max(s_b, axis=1, keepdims=True), jnp.max(s_n, axis=1, keepdims=True))
    p_b = jnp.exp(s_b - m) * mb_ref[...]
    p_n = jnp.exp(s_n - m) * mn_ref[...]
    den = jnp.sum(p_b, axis=1, keepdims=True) + jnp.sum(p_n, axis=1, keepdims=True)
    num = _dot_nt(p_b.astype(BF16), vt_ref[...].astype(BF16)) + _dot(p_n.astype(BF16), vn)
    full = jnp.where(same, num / den, 0.0)
    o = full[0:ts]
    for h in range(1, SWA_HEADS):
        o = o + full[h * ts:(h + 1) * ts]
    o_ref[...] = o


def _attn_sample_operands(qs, kn, vn, kbt, vbt, ts, seq_of):
    Bd, _, wb = kbt.shape
    for (W, d) in DILATED_PATTERNS:
        assert wb - (W // d) * d >= 0, "window buffer shorter than a pattern's reach"
    tables = [jnp.asarray(t) for t in _sample_tables(wb, ts)]
    row = pl.BlockSpec((ts, SW), lambda *ids: (seq_of(*ids), 0))
    buf = pl.BlockSpec((None, SW, wb), lambda *ids: (seq_of(*ids), 0, 0))
    operands = [qs, kn, vn, kbt, vbt] + tables
    specs = [row, row, row, buf, buf] + [_const_spec(t.shape) for t in tables]
    return operands, specs, row, jax.ShapeDtypeStruct((Bd * ts, SW), F32)


def _ffn_body(og_ref, os_ref, x_ref, wo_ref, wfg_ref, wfu_ref, wfd_ref, gf_ref, gl_ref, y_ref,
              *scratch, os_slabs):
    if os_slabs:
        stage, = scratch
        for r in range(RES):
            for sl in range(SW // LANES):
                stage[sl, pl.ds(r, SUB, stride=RES), :] = (
                    os_ref[r, :, sl * LANES:(sl + 1) * LANES].astype(F32))
        osw = jnp.concatenate([stage[sl] for sl in range(SW // LANES)], axis=1).astype(BF16)
    else:
        osw = os_ref[...].astype(BF16)
    og = og_ref[...].astype(BF16)
    h = x_ref[...] + _dot(og, wo_ref[0:GV, :]) + _dot(osw, wo_ref[GV:GV + SW, :])
    hn = _rms(h, gf_ref[...]).astype(BF16)
    acc = jnp.zeros(h.shape, F32)
    for c in range(D_FF // FF_CHUNK):
        cs = slice(c * FF_CHUNK, (c + 1) * FF_CHUNK)
        gate = _dot(hn, wfg_ref[:, cs])
        up = _dot(hn, wfu_ref[:, cs])
        act = (gate * _sigmoid(gate) * up).astype(BF16)
        acc = acc + _dot(act, wfd_ref[cs, :])
    y_ref[...] = _rms(h + acc, gl_ref[...])


def _ffn_weights(w_out, g_ffn, w_fg, w_fu, w_fd, g_final):
    return (w_out.astype(BF16), w_fg.astype(BF16), w_fu.astype(BF16), w_fd.astype(BF16),
            g_ffn.reshape(1, D_MODEL), g_final.reshape(1, D_MODEL))


def _ffn_prompt(og, o16, x, wts):
    B, T, _ = x.shape
    per = TILE // ROWS
    wo, wfg, wfu, wfd, gf, gl = wts
    tok = lambda c: pl.BlockSpec((None, ROWS, c), lambda b, i: (b, i, 0))
    slab = pl.BlockSpec((None, None, RES, SUB, SW), lambda b, i: (b, i // per, 0, i % per, 0))
    return pl.pallas_call(
        functools.partial(_ffn_body, os_slabs=True),
        grid=(B, T // ROWS),
        in_specs=[tok(GV), slab, tok(D_MODEL),
                  _resident(wo.shape), _resident(wfg.shape), _resident(wfu.shape),
                  _resident(wfd.shape), _const_spec(gf.shape), _const_spec(gl.shape)],
        out_specs=tok(D_MODEL),
        out_shape=jax.ShapeDtypeStruct((B, T, D_MODEL), F32),
        scratch_shapes=[pltpu.VMEM((SW // LANES, ROWS, LANES), F32)],
        compiler_params=_params(2),
        name="ffn_prompt",
    )(og, o16, x, wo, wfg, wfu, wfd, gf, gl)


def _ffn_sample(og, osw, x, wts):
    N = x.shape[0]
    wo, wfg, wfu, wfd, gf, gl = wts
    spec = lambda c: pl.BlockSpec((ROWS, c), lambda i: (i, 0))
    return pl.pallas_call(
        functools.partial(_ffn_body, os_slabs=False),
        grid=(N // ROWS,),
        in_specs=[spec(GV), spec(SW), spec(D_MODEL),
                  _resident(wo.shape), _resident(wfg.shape), _resident(wfu.shape),
                  _resident(wfd.shape), _const_spec(gf.shape), _const_spec(gl.shape)],
        out_specs=spec(D_MODEL),
        out_shape=jax.ShapeDtypeStruct((N, D_MODEL), F32),
        compiler_params=_params(1),
        name="ffn_sample",
    )(og, osw, x, wo, wfg, wfu, wfd, gf, gl)


def _layer(xp, xs, s0, kbuf, vbuf, w_in, w_gup, b_gate, g_mix, g_gla, w_out, g_ffn,
           w_fg, w_fu, w_fd, g_last):
    B, T, _ = xp.shape
    Bd, Ts, _ = xs.shape
    N = Bd * Ts
    assert T % TILE == 0 and T >= WIN_MAX and WIN_MAX % ROWS == 0
    assert CHUNK % Ts == 0 and N % ROWS == 0
    w, wg, bg, gm = _pack_weights(w_in, w_gup, b_gate, g_mix)
    wts = _ffn_weights(w_out, g_ffn, w_fg, w_fu, w_fd, g_last)
    xs2 = xs.reshape(N, D_MODEL)

    qg, kg, vg, r, la, qs, ks, vs = _inproj_sample(xs2, w, wg, bg, gm)
    ogs, ss = _gla_sample(qg, kg, vg, la, r, g_gla, s0, Ts)
    wb = kbuf.shape[1]
    feat = lambda a: jnp.transpose(a, (0, 2, 3, 1)).reshape(Bd, SW, wb)

    qg, kg, vg, r, la, q16, k16, v16, kt, vt, kk, vk = _inproj_prompt(xp, w, wg, bg, gm)
    og, sp, oss = _gla_prompt(qg, kg, vg, la, r, g_gla, (qs, ks, vs, feat(kbuf), feat(vbuf), Ts))
    o16 = _attn_prompt(q16, k16, v16, kt, vt)
    yp = _ffn_prompt(og, o16, xp, wts)
    ys = _ffn_sample(ogs, oss, xs2, wts)
    keep = lambda a: jnp.transpose(a.reshape(B, SWA_HEADS, SWA_HD, WIN_MAX), (0, 3, 1, 2))
    new = lambda a: a.reshape(Bd, Ts, SWA_HEADS, SWA_HD)
    return yp, ys.reshape(Bd, Ts, D_MODEL), sp, ss, keep(kk), keep(vk), new(ks), new(vs)


def kernel(x_prompt, x_sample, state_gla, cache_swa_k, cache_swa_v, w_in, w_gate_up, b_gate,
           g_mix_norm, g_gla_norm, w_out, g_ffn_norm, w_ffn_gate, w_ffn_up, w_ffn_down, g_final):
    depth = w_in.shape[0]
    assert depth == 1, "the final norm is fused into the single layer's FFN kernel"
    outs = _layer(x_prompt, x_sample, state_gla[0], cache_swa_k[0], cache_swa_v[0],
                  w_in[0], w_gate_up[0], b_gate[0], g_mix_norm[0], g_gla_norm[0], w_out[0],
                  g_ffn_norm[0], w_ffn_gate[0], w_ffn_up[0], w_ffn_down[0], g_final)
    yp, ys = outs[:2]
    return (yp, ys) + tuple(o[None] for o in outs[2:])
```

```python
import functools

import numpy as np
import jax
import jax.numpy as jnp
from jax import lax
from jax.experimental import pallas as pl
from jax.experimental.pallas import tpu as pltpu

F32 = jnp.float32
BF16 = jnp.bfloat16

D_MODEL = 1024
GLA_HEADS = 4
GLA_DK = 64
GLA_DV = 128
GLA_RANK = 16
GLA_TEMP = 16.0
SWA_HEADS = 8
SWA_HD = 64
DILATED_PATTERNS = ((128, 1), (512, 4), (2048, 16))
WIN_MAX = 2048
D_FF = 2816
EPS = 1e-6

GK = GLA_HEADS * GLA_DK
GV = GLA_HEADS * GLA_DV
SW = SWA_HEADS * SWA_HD
LANES = 128
TILE = 2048
RES = 16
BLK = 128
CHUNK = 64
ROWS = 512
SUB = ROWS // RES
NEG = -1e30
LOG2E = float(np.log2(np.e))
FF_CHUNK = 256
C_QG, C_KG, C_VG, C_R, C_QS, C_KS, C_VS, C_LR, C_END = 0, 256, 512, 1024, 1536, 2048, 2560, 3072, 3200
VMEM_LIMIT = 56 * 1024 * 1024


def _rms(x, g):
    ms = jnp.mean(x * x, axis=-1, keepdims=True)
    return x * lax.rsqrt(ms + EPS) * g


def _sigmoid(x):
    return 1.0 / (1.0 + jnp.exp(-x))


def _dot(a, b):
    return jnp.dot(a, b, preferred_element_type=F32)


def _dot_nt(a, b):
    return lax.dot_general(a, b, (((1,), (1,)), ((), ())), preferred_element_type=F32)


def _dot_tn(a, b):
    return lax.dot_general(a, b, (((0,), (0,)), ((), ())), preferred_element_type=F32)


def _const_spec(shape):
    nd = len(shape)
    return pl.BlockSpec(shape, lambda *_: (0,) * nd)


def _resident(shape):
    nd = len(shape)
    return pl.BlockSpec(shape, lambda *_: (0,) * nd, pipeline_mode=pl.Buffered(1))


def _params(n_axes):
    return pltpu.CompilerParams(dimension_semantics=("arbitrary",) * n_axes,
                                vmem_limit_bytes=VMEM_LIMIT)


def _inproj_body(x_ref, w_ref, wg_ref, bg_ref, gm_ref,
                 qg_ref, kg_ref, vg_ref, r_ref, la_ref, *rest, prompt, keep_first):
    xn = _rms(x_ref[...], gm_ref[...]).astype(BF16)

    def proj(lo, hi):
        return _dot(xn, w_ref[:, lo:hi])

    def gla_qk():
        qg_ref[...] = proj(C_QG, C_KG) * (GLA_DK ** -0.5)
        kg_ref[...] = proj(C_KG, C_VG)

    def gla_gate():
        lr = proj(C_LR, C_END)
        z = _dot(lr.astype(BF16), wg_ref[...]) + bg_ref[...]
        log_sig = -(jnp.maximum(-z, 0.0) + jnp.log1p(jnp.exp(-jnp.abs(z))))
        la_ref[...] = log_sig * (LOG2E / GLA_TEMP)

    def gla_rest():
        vg_ref[...] = proj(C_VG, C_R).astype(vg_ref.dtype)
        r_ref[...] = proj(C_R, C_QS)

    gla_gate()
    if not prompt:
        qs_ref, ks_ref, vs_ref = rest
        qs_ref[...] = proj(C_QS, C_KS) * (SWA_HD ** -0.5)
        ks_ref[...] = proj(C_KS, C_VS)
        vs_ref[...] = proj(C_VS, C_LR)
        gla_qk()
        gla_rest()
        return
    q16_ref, k16_ref, v16_ref, kt_ref, vt_ref, kk_ref, vk_ref, stage, stage4 = rest

    def to_slabs(val, ref, buf):
        for sl in range(SW // LANES):
            stage[buf, sl] = val[:, sl * LANES:(sl + 1) * LANES]
        for sl in range(SW // LANES):
            for r4 in range(4):
                stage4[buf, sl, r4] = stage[buf, sl, pl.ds(r4, ROWS // 4, stride=4), :]
            for r4 in range(4):
                for c in range(4):
                    ref[4 * c + r4, :, sl * LANES:(sl + 1) * LANES] = (
                        stage4[buf, sl, r4, pl.ds(c, SUB, stride=4), :].astype(ref.dtype))

    k = proj(C_KS, C_VS)
    kt_ref[...] = k.astype(BF16)
    v = proj(C_VS, C_LR)
    to_slabs(k, k16_ref, 0)
    vt_ref[...] = v.astype(BF16)
    q = proj(C_QS, C_KS) * (SWA_HD ** -0.5 * LOG2E)
    to_slabs(v, v16_ref, 1)
    gla_qk()
    to_slabs(q, q16_ref, 2)
    gla_rest()

    @pl.when(pl.program_id(1) >= keep_first)
    def _():
        for sl in range(SW // LANES):
            kk_ref[sl * LANES:(sl + 1) * LANES, :] = stage[0, sl].T
            vk_ref[sl * LANES:(sl + 1) * LANES, :] = stage[1, sl].T


def _pack_weights(w_in, w_gup, b_gate, g_mix):
    sizes = (GK, GK, GV, GV, GLA_RANK, SW, SW, SW)
    offs = np.concatenate([[0], np.cumsum(sizes)])
    qg, kg, vg, r, lr, qs, ks, vs = [w_in[:, offs[i]:offs[i + 1]] for i in range(8)]
    lr = jnp.pad(lr, ((0, 0), (0, LANES - GLA_RANK)))
    w = jnp.concatenate([qg, kg, vg, r, qs, ks, vs, lr], axis=1).astype(BF16)
    wg = jnp.pad(w_gup, ((0, LANES - GLA_RANK), (0, 0))).astype(BF16)
    return w, wg, b_gate.reshape(1, GK), g_mix.reshape(1, D_MODEL)


def _inproj_prompt(x, w, wg, bg, gm):
    B, T, _ = x.shape
    nt, per = T // TILE, TILE // ROWS
    n = T // ROWS
    keep_first = n - WIN_MAX // ROWS
    tok = lambda c: pl.BlockSpec((None, ROWS, c), lambda b, i: (b, i, 0))
    slab = pl.BlockSpec((None, None, RES, SUB, SW), lambda b, i: (b, i // per, 0, i % per, 0))
    keep = pl.BlockSpec((None, SW, ROWS), lambda b, i: (b, 0, jnp.maximum(i - keep_first, 0)))
    tshape = lambda c, dt: jax.ShapeDtypeStruct((B, T, c), dt)
    sshape = lambda dt: jax.ShapeDtypeStruct((B, nt, RES, BLK, SW), dt)
    kshape = jax.ShapeDtypeStruct((B, SW, WIN_MAX), F32)
    return pl.pallas_call(
        functools.partial(_inproj_body, prompt=True, keep_first=keep_first),
        grid=(B, n),
        in_specs=[tok(D_MODEL), _resident(w.shape), _const_spec(wg.shape),
                  _const_spec(bg.shape), _const_spec(gm.shape)],
        out_specs=(tok(GK), tok(GK), tok(GV), tok(GV), tok(GK),
                   slab, slab, slab, tok(SW), tok(SW), keep, keep),
        out_shape=(tshape(GK, F32), tshape(GK, F32), tshape(GV, BF16), tshape(GV, F32),
                   tshape(GK, F32), sshape(F32), sshape(BF16), sshape(BF16),
                   tshape(SW, BF16), tshape(SW, BF16), kshape, kshape),
        scratch_shapes=[pltpu.VMEM((3, SW // LANES, ROWS, LANES), F32),
                        pltpu.VMEM((3, SW // LANES, 4, ROWS // 4, LANES), F32)],
        compiler_params=_params(2),
        name="inproj_prompt",
    )(x, w, wg, bg, gm)


def _inproj_sample(x, w, wg, bg, gm):
    N = x.shape[0]
    spec = lambda c: pl.BlockSpec((ROWS, c), lambda i: (i, 0))
    shp = lambda c: jax.ShapeDtypeStruct((N, c), F32)
    return pl.pallas_call(
        functools.partial(_inproj_body, prompt=False, keep_first=0),
        grid=(N // ROWS,),
        in_specs=[spec(D_MODEL), _resident(w.shape), _const_spec(wg.shape),
                  _const_spec(bg.shape), _const_spec(gm.shape)],
        out_specs=(spec(GK), spec(GK), spec(GV), spec(GV), spec(GK), spec(SW), spec(SW), spec(SW)),
        out_shape=(shp(GK), shp(GK), shp(GV), shp(GV), shp(GK), shp(SW), shp(SW), shp(SW)),
        compiler_params=_params(1),
        name="inproj_sample",
    )(x, w, wg, bg, gm)


def _gla_masks(C, seg):
    ti = lax.broadcasted_iota(jnp.int32, (C, C), 0)
    si = lax.broadcasted_iota(jnp.int32, (C, C), 1)
    rows = lax.broadcasted_iota(jnp.int32, (C, GK), 0)
    masks, signs = [ti == si], [None]
    h, sh = 1, 0
    while h < seg:
        masks.append(jnp.logical_and(((ti ^ si) >> sh) == 1, si < ti))
        signs.append(jnp.where((rows & (2 * h - 1)) >= h, 1.0, -1.0).astype(F32))
        h, sh = 2 * h, sh + 1
    return masks, signs


def _gla_chunk(q, k, v, la, r, g, seg, states, consts):
    C = q.shape[0]
    nseg = C // seg
    rows = lax.broadcasted_iota(jnp.int32, (C, GK), 0)
    rin = rows & (seg - 1)
    b_inc = la
    d = 1
    while d < seg:
        b_inc = b_inc + jnp.where(rin >= d, pltpu.roll(b_inc, d, 0), 0.0)
        d *= 2

    lane = lax.broadcasted_iota(jnp.int32, (C, LANES), 1)
    head_lo = lane < GLA_DK
    row_c = lax.broadcasted_iota(jnp.int32, (C, LANES), 0)

    def pair(a, p):
        return a[:, p * LANES:(p + 1) * LANES]

    def by_head(a):
        return jnp.concatenate([jnp.where(head_lo, a, 0.0), jnp.where(head_lo, 0.0, a)], axis=0)

    A = [jnp.zeros((C, C), F32) for _ in range(GLA_HEADS)]

    def add_level(qs, ks, mask):
        for p in range(2):
            a2 = _dot_nt(by_head(pair(qs, p)).astype(BF16), pair(ks, p).astype(BF16))
            A[2 * p] = jnp.where(mask, a2[:C], A[2 * p])
            A[2 * p + 1] = jnp.where(mask, a2[C:], A[2 * p + 1])

    masks, signs = consts
    SUBL = 8
    last8 = jnp.broadcast_to(b_inc.reshape(C // SUBL, SUBL, GK)[:, SUBL - 1:, :],
                             (C // SUBL, SUBL, GK))

    def group_end(h):
        n = C // (2 * h)
        picked = last8.reshape(n, 2 * h // SUBL, SUBL, GK)[:, h // SUBL - 1:h // SUBL]
        return jnp.broadcast_to(picked, (n, 2 * h // SUBL, SUBL, GK)).reshape(C, GK)

    add_level(q, k, masks[0])
    end = b_inc
    h, lvl = 1, 1
    while h < seg:
        if h >= SUBL:
            bound = group_end(h)
        else:
            upper = (rin & (2 * h - 1)) >= h
            bound = jnp.where(upper, pltpu.roll(end, h, 0), end)
            end = jnp.where(upper, end, pltpu.roll(end, C - h, 0))
        fac = jnp.exp2((b_inc - bound) * signs[lvl])
        add_level(q * fac, k * fac, masks[lvl])
        h, lvl = 2 * h, lvl + 1
    if seg >= SUBL:
        end = jnp.broadcast_to(last8.reshape(nseg, seg // SUBL, SUBL, GK)[:, seg // SUBL - 1:],
                               (nseg, seg // SUBL, SUBL, GK)).reshape(C, GK)
    q_int = q * jnp.exp2(b_inc)
    k_upd = k * jnp.exp2(end - b_inc)

    def seg_rows(j):
        return jnp.logical_and(row_c >= j * seg, row_c < (j + 1) * seg)

    outs = []
    for p in range(2):
        qp = pair(q_int, p)
        inter = jnp.zeros((2 * C, LANES), F32)
        for j in range(nseg):
            qj = qp if nseg == 1 else jnp.where(seg_rows(j), qp, 0.0)
            inter = inter + _dot(by_head(qj).astype(BF16), states[j][p].astype(BF16))
        for hp in range(2):
            hd = 2 * p + hp
            vh = v[:, hd * GLA_DV:(hd + 1) * GLA_DV]
            o = _dot(A[hd].astype(BF16), vh) + inter[hp * C:(hp + 1) * C]
            gh = g[:, hd * GLA_DV:(hd + 1) * GLA_DV]
            rh = r[:, hd * GLA_DV:(hd + 1) * GLA_DV]
            outs.append(_rms(o, gh) * (rh * _sigmoid(rh)))

    r128 = lax.broadcasted_iota(jnp.int32, (LANES, LANES), 0)
    c128 = lax.broadcasted_iota(jnp.int32, (LANES, LANES), 1)
    new_states = []
    for j in range(nseg):
        st = []
        for p in range(2):
            kp = pair(k_upd, p)
            if nseg > 1:
                kp = jnp.where(seg_rows(j), kp, 0.0)
            kp = kp.astype(BF16)
            t0 = _dot_tn(kp, v[:, (2 * p) * GLA_DV:(2 * p + 1) * GLA_DV])
            t1 = _dot_tn(kp, v[:, (2 * p + 1) * GLA_DV:(2 * p + 2) * GLA_DV])
            t = jnp.where(r128 < GLA_DK, t0, t1)
            e_row = jnp.exp2(pair(end, p)[j * seg:j * seg + 1, :])
            e_col = jnp.sum(jnp.where(r128 == c128, jnp.broadcast_to(e_row, (LANES, LANES)), 0.0),
                            axis=1, keepdims=True)
            st.append(e_col * states[j][p] + t)
        new_states.append(st)
    return jnp.concatenate(outs, axis=1), new_states


N_RIDER_IN = 9


def _gla_prompt_body(q_ref, k_ref, v_ref, la_ref, r_ref, g_ref, *rest, n_chunks, rider_ts):
    rider_in, (o_ref, s_ref, so_ref, s_scr) = rest[:N_RIDER_IN], rest[N_RIDER_IN:]
    i = pl.program_id(1)

    @pl.when(i == 0)
    def _():
        s_scr[...] = jnp.zeros_like(s_scr)

    _attn_sample_body(*rider_in, so_ref, ts=rider_ts)
    g = g_ref[...]
    masks = _gla_masks(CHUNK, CHUNK)

    def step(c, carry):
        sl = pl.ds(pl.multiple_of(c * CHUNK, CHUNK), CHUNK)
        o, st = _gla_chunk(q_ref[sl, :], k_ref[sl, :], v_ref[sl, :], la_ref[sl, :], r_ref[sl, :],
                           g, CHUNK, [[s_scr[0], s_scr[1]]], masks)
        o_ref[sl, :] = o.astype(o_ref.dtype)
        s_scr[0] = st[0][0]
        s_scr[1] = st[0][1]
        return carry

    lax.fori_loop(0, n_chunks, step, 0, unroll=True)

    @pl.when(i == pl.num_programs(1) - 1)
    def _():
        s_ref[...] = s_scr[...]


def _gla_prompt(qg, kg, vg, la, r, g_gla, sample):
    B, T, _ = qg.shape
    n = T // ROWS
    assert sample[3].shape[0] == B * n, "one sample sequence per prompt GLA grid step"
    spec = lambda c: pl.BlockSpec((None, ROWS, c), lambda b, i: (b, i, 0))
    r_ops, r_specs, r_out_spec, r_out_shape = _attn_sample_operands(*sample, lambda b, i: b * n + i)
    assert len(r_ops) == N_RIDER_IN
    og, s, oss = pl.pallas_call(
        functools.partial(_gla_prompt_body, n_chunks=ROWS // CHUNK, rider_ts=sample[5]),
        grid=(B, n),
        in_specs=[spec(GK), spec(GK), spec(GV), spec(GK), spec(GV), _const_spec((1, GV))] + r_specs,
        out_specs=(spec(GV), pl.BlockSpec((None, 2, LANES, LANES), lambda b, i: (b, 0, 0, 0)),
                   r_out_spec),
        out_shape=(jax.ShapeDtypeStruct((B, T, GV), BF16),
                   jax.ShapeDtypeStruct((B, 2, LANES, LANES), F32), r_out_shape),
        scratch_shapes=[pltpu.VMEM((2, LANES, LANES), F32)],
        compiler_params=_params(2),
        name="gla_prompt",
    )(qg, kg, vg, la, r, g_gla.reshape(1, GV), *r_ops)
    return og, s.reshape(B, GLA_HEADS, GLA_DK, GLA_DV), oss


def _gla_sample_body(q_ref, k_ref, v_ref, la_ref, r_ref, g_ref, s0_ref, o_ref, s_ref, *, seg):
    nseg = CHUNK // seg
    states = [[s0_ref[j, 0], s0_ref[j, 1]] for j in range(nseg)]
    o, st = _gla_chunk(q_ref[...], k_ref[...], v_ref[...].astype(BF16), la_ref[...], r_ref[...],
                       g_ref[...], seg, states, _gla_masks(CHUNK, seg))
    o_ref[...] = o
    for j in range(nseg):
        s_ref[j, 0] = st[j][0]
        s_ref[j, 1] = st[j][1]


def _gla_sample(qg, kg, vg, la, r, g_gla, s0, seg):
    N = qg.shape[0]
    nb = CHUNK // seg
    nseq = N // seg
    spec = lambda c: pl.BlockSpec((CHUNK, c), lambda i: (i, 0))
    sspec = pl.BlockSpec((nb, 2, LANES, LANES), lambda i: (i, 0, 0, 0))
    og, s = pl.pallas_call(
        functools.partial(_gla_sample_body, seg=seg),
        grid=(N // CHUNK,),
        in_specs=[spec(GK), spec(GK), spec(GV), spec(GK), spec(GV), _const_spec((1, GV)), sspec],
        out_specs=(spec(GV), sspec),
        out_shape=(jax.ShapeDtypeStruct((N, GV), F32),
                   jax.ShapeDtypeStruct((nseq, 2, LANES, LANES), F32)),
        compiler_params=_params(1),
        name="gla_sample",
    )(qg, kg, vg, la, r, g_gla.reshape(1, GV), s0.reshape(nseq, 2, LANES, LANES))
    return og, s.reshape(nseq, GLA_HEADS, GLA_DK, GLA_DV)


def _alibi_slopes():
    return np.exp2(-8.0 * (np.arange(SWA_HEADS, dtype=np.float64) + 1.0) / SWA_HEADS)


def _prompt_bias_tables():
    slopes = _alibi_slopes()
    idx = np.arange(BLK)
    q_nat = {16: idx, 4: 4 * (idx % 32) + idx // 32, 1: 16 * (idx % 8) + idx // 8}
    k_nat = {16: idx, 4: 4 * (idx % 32) + idx // 32, 1: idx}
    out = np.empty((3, 2, SWA_HEADS, BLK, 2 * BLK), np.float32)
    for pi, d in enumerate((16, 4, 1)):
        j = np.concatenate([k_nat[d], BLK + k_nat[d]])[None, :]
        stp = q_nat[d][:, None] + BLK - j
        band = (stp >= 0) & (stp <= BLK)
        for first in (0, 1):
            valid = band & ((j >= BLK) if first else True)
            bias = -LOG2E * slopes[:, None, None] * (stp * d).astype(np.float64)[None]
            out[pi, first] = np.where(valid[None], bias, NEG).astype(np.float32)
    return out


def _attend(q, kk, vv, bias_of, prev):
    lo = lax.broadcasted_iota(jnp.int32, (BLK, LANES), 1) < SWA_HD
    lo_k = lax.broadcasted_iota(jnp.int32, (2 * BLK, LANES), 1) < SWA_HD
    zero, one = jnp.zeros_like(q), jnp.ones_like(vv)
    q2 = jnp.concatenate([jnp.where(lo, q, zero), jnp.where(lo, zero, q)], axis=0)
    s2 = _dot_nt(q2, kk)
    m_new, res, alphas = [], [], []
    for hp in range(2):
        s = s2[hp * BLK:(hp + 1) * BLK] + bias_of(hp)
        mc = jnp.max(s, axis=1, keepdims=True)
        if prev is None:
            mn = jnp.broadcast_to(mc, (BLK, LANES))
        else:
            mn = jnp.maximum(prev[0][hp], mc)
            alphas.append(jnp.exp2(prev[0][hp] - mn))
        p = jnp.exp2(s - jnp.concatenate([mn, mn], axis=1)).astype(BF16)
        w = jnp.where(lo_k, vv, one) if hp == 0 else jnp.where(lo_k, one, vv)
        res.append(_dot(p, w))
        m_new.append(mn)
    acc = jnp.where(lo, res[0], res[1])
    l = jnp.where(lo, res[1], res[0])
    if prev is not None:
        acc = jnp.where(lo, alphas[0], alphas[1]) * prev[2] + acc
        l = jnp.where(lo, alphas[1], alphas[0]) * prev[1] + l
    return m_new, l, acc


def _attn_prompt_body(q_ref, kc_ref, vc_ref, kp_ref, vp_ref, ktc_ref, vtc_ref, ktp_ref, vtp_ref,
                      bias_ref, o_ref, m_scr, l_scr, a_scr, tk_scr, tv_scr):
    tile0 = (pl.program_id(2) == 0).astype(jnp.int32)
    tk_scr[0:BLK, :] = ktp_ref[...]
    tk_scr[BLK:, :] = ktc_ref[...]
    tv_scr[0:BLK, :] = vtp_ref[...]
    tv_scr[BLK:, :] = vtc_ref[...]

    for r in range(RES):
        kk = jnp.concatenate([kp_ref[r], kc_ref[r]], axis=0)
        vv = jnp.concatenate([vp_ref[r], vc_ref[r]], axis=0)
        m, l, a = _attend(q_ref[r].astype(BF16), kk, vv, lambda hp: bias_ref[0, tile0, hp], None)
        for hp in range(2):
            m_scr[hp, r] = m[hp]
        l_scr[r] = l
        a_scr[r] = a

    for r4 in range(4):
        for kb in range(4):
            rows = slice(32 * kb, 32 * kb + 32)

            def keys(cur, prev):
                if kb == 0:
                    before = [prev[4 * c + r4, 96:128, :] for c in range(4)]
                else:
                    before = [cur[4 * c + r4, 32 * kb - 32:32 * kb, :] for c in range(4)]
                return jnp.concatenate(before + [cur[4 * c + r4, rows, :] for c in range(4)], axis=0)

            cat = lambda ref, *i: jnp.concatenate([ref[(*i, 4 * c + r4, rows)] for c in range(4)], axis=0)
            first = tile0 if kb == 0 else 0
            prev = ([cat(m_scr, hp) for hp in range(2)], cat(l_scr), cat(a_scr))
            m, l, a = _attend(cat(q_ref).astype(BF16), keys(kc_ref, kp_ref), keys(vc_ref, vp_ref),
                              lambda hp: bias_ref[1, first, hp], prev)
            for c in range(4):
                cs = slice(32 * c, 32 * c + 32)
                for hp in range(2):
                    m_scr[hp, 4 * c + r4, rows, :] = m[hp][cs]
                l_scr[4 * c + r4, rows, :] = l[cs]
                a_scr[4 * c + r4, rows, :] = a[cs]

    for j in range(RES):
        rs = slice(j * 8, j * 8 + 8)
        ks = slice(j * BLK, j * BLK + 2 * BLK)
        cat = lambda ref, *idx: ref[(*idx, slice(None), rs, slice(None))].reshape(BLK, LANES)
        first = tile0 if j == 0 else 0
        prev = ([cat(m_scr, hp) for hp in range(2)], cat(l_scr), cat(a_scr))
        m, l, a = _attend(cat(q_ref).astype(BF16), tk_scr[ks, :], tv_scr[ks, :],
                          lambda hp: bias_ref[2, first, hp], prev)
        for hp in range(2):
            m_scr[hp, :, rs, :] = m[hp].reshape(RES, 8, LANES)
        l_scr[:, rs, :] = l.reshape(RES, 8, LANES)
        a_scr[:, rs, :] = a.reshape(RES, 8, LANES)

    for r in range(RES):
        o_ref[r] = (a_scr[r] / pltpu.roll(l_scr[r], SWA_HD, 1)).astype(o_ref.dtype)


def _attn_prompt(q16, k16, v16, kt, vt):
    B, nt = q16.shape[0], q16.shape[1]
    bias = jnp.asarray(_prompt_bias_tables())
    tile = (None, None, RES, BLK, LANES)
    cur = pl.BlockSpec(tile, lambda g, b, a: (b, a, 0, 0, g))
    prv = pl.BlockSpec(tile, lambda g, b, a: (b, jnp.maximum(a - 1, 0), 0, 0, g))
    tcur = pl.BlockSpec((None, TILE, LANES), lambda g, b, a: (b, a, g))
    tprv = pl.BlockSpec((None, BLK, LANES),
                        lambda g, b, a: (b, jnp.maximum(a * (TILE // BLK) - 1, 0), g))
    bias_spec = pl.BlockSpec((3, 2, 2, BLK, 2 * BLK), lambda g, b, a: (0, 0, g, 0, 0))
    return pl.pallas_call(
        _attn_prompt_body,
        grid=(SWA_HEADS // 2, B, nt),
        in_specs=[cur, cur, cur, prv, prv, tcur, tcur, tprv, tprv, bias_spec],
        out_specs=cur,
        out_shape=jax.ShapeDtypeStruct((B, nt, RES, BLK, SW), BF16),
        scratch_shapes=[pltpu.VMEM((2, RES, BLK, LANES), F32),
                        pltpu.VMEM((RES, BLK, LANES), F32),
                        pltpu.VMEM((RES, BLK, LANES), F32),
                        pltpu.VMEM((TILE + BLK, LANES), BF16),
                        pltpu.VMEM((TILE + BLK, LANES), BF16)],
        compiler_params=_params(3),
        name="attn_prompt",
    )(q16, k16, v16, k16, v16, kt, vt, kt, vt, bias)


def _sample_tables(wb, ts):
    slopes = _alibi_slopes()
    ncol = wb + LANES
    t = np.arange(ts)[:, None]
    c = np.arange(ncol)[None, :]
    dist = wb + t - c
    mult = np.zeros((ts, ncol), np.float64)
    for (W, d) in DILATED_PATTERNS:
        mult += (dist >= 0) & (dist % d == 0) & (dist // d <= W // d) & (c < wb + ts)
    bias = np.where(mult[None] > 0, -slopes[:, None, None] * dist[None].astype(np.float64), NEG)
    mult = np.broadcast_to(mult[None], bias.shape)
    shp = (SWA_HEADS * ts, ncol)
    bias = bias.reshape(shp).astype(np.float32)
    mult = mult.reshape(shp).astype(np.float32)
    return bias[:, :wb], mult[:, :wb], bias[:, wb:], mult[:, wb:]


def _attn_sample_body(q_ref, kn_ref, vn_ref, kt_ref, vt_ref, bb_ref, mb_ref, bn_ref, mn_ref, o_ref,
                      *, ts):
    nr = SWA_HEADS * ts
    q = q_ref[...]
    qt = jnp.concatenate([q] * SWA_HEADS, axis=0)
    rh = lax.broadcasted_iota(jnp.int32, (nr, SW), 0) // ts
    ch = lax.broadcasted_iota(jnp.int32, (nr, SW), 1) // SWA_HD
    same = rh == ch
    qbd = jnp.where(same, qt, 0.0).astype(BF16)
    zpad = jnp.zeros((LANES - ts, SW), F32)
    kn = jnp.concatenate([kn_ref[...], zpad], axis=0).astype(BF16)
    vn = jnp.concatenate([vn_ref[...], zpad], axis=0).astype(BF16)
    s_b = _dot(qbd, kt_ref[...].astype(BF16)) + bb_ref[...]
    s_n = _dot_nt(qbd, kn) + bn_ref[...]
    m = jnp.maximum(jnp.max(s_b, axis=1, keepdims=True), jnp.max(s_n, axis=1, keepdims=True))
    p_b = jnp.exp(s_b - m) * mb_ref[...]
    p_n = jnp.exp(s_n - m) * mn_ref[...]
    den = jnp.sum(p_b, axis=1, keepdims=True) + jnp.sum(p_n, axis=1, keepdims=True)
    num = _dot_nt(p_b.astype(BF16), vt_ref[...].astype(BF16)) + _dot(p_n.astype(BF16), vn)
    full = jnp.where(same, num / den, 0.0)
    o = full[0:ts]
    for h in range(1, SWA_HEADS):
        o = o + full[h * ts:(h + 1) * ts]
    o_ref[...] = o


def _attn_sample_operands(qs, kn, vn, kbt, vbt, ts, seq_of):
    Bd, _, wb = kbt.shape
    for (W, d) in DILATED_PATTERNS:
        assert wb - (W // d) * d >= 0, "window buffer shorter than a pattern's reach"
    tables = [jnp.asarray(t) for t in _sample_tables(wb, ts)]
    row = pl.BlockSpec((ts, SW), lambda *ids: (seq_of(*ids), 0))
    buf = pl.BlockSpec((None, SW, wb), lambda *ids: (seq_of(*ids), 0, 0))
    operands = [qs, kn, vn, kbt, vbt] + tables
    specs = [row, row, row, buf, buf] + [_const_spec(t.shape) for t in tables]
    return operands, specs, row, jax.ShapeDtypeStruct((Bd * ts, SW), F32)


def _ffn_body(og_ref, os_ref, x_ref, wo_ref, wfg_ref, wfu_ref, wfd_ref, gf_ref, gl_ref, y_ref,
              *scratch, os_slabs):
    if os_slabs:
        stage, stage4 = scratch
        for sl in range(SW // LANES):
            for r4 in range(4):
                for c in range(4):
                    stage4[sl, r4, pl.ds(c, SUB, stride=4), :] = (
                        os_ref[4 * c + r4, :, sl * LANES:(sl + 1) * LANES].astype(F32))
            for r4 in range(4):
                stage[sl, pl.ds(r4, ROWS // 4, stride=4), :] = stage4[sl, r4]
        osw = jnp.concatenate([stage[sl] for sl in range(SW // LANES)], axis=1).astype(BF16)
    else:
        osw = os_ref[...].astype(BF16)
    og = og_ref[...].astype(BF16)
    h = x_ref[...] + _dot(og, wo_ref[0:GV, :]) + _dot(osw, wo_ref[GV:GV + SW, :])
    hn = _rms(h, gf_ref[...]).astype(BF16)
    acc = jnp.zeros(h.shape, F32)
    for c in range(D_FF // FF_CHUNK):
        cs = slice(c * FF_CHUNK, (c + 1) * FF_CHUNK)
        gate = _dot(hn, wfg_ref[:, cs])
        up = _dot(hn, wfu_ref[:, cs])
        act = (gate * _sigmoid(gate) * up).astype(BF16)
        acc = acc + _dot(act, wfd_ref[cs, :])
    y_ref[...] = _rms(h + acc, gl_ref[...])


def _ffn_weights(w_out, g_ffn, w_fg, w_fu, w_fd, g_final):
    return (w_out.astype(BF16), w_fg.astype(BF16), w_fu.astype(BF16), w_fd.astype(BF16),
            g_ffn.reshape(1, D_MODEL), g_final.reshape(1, D_MODEL))


def _ffn_prompt(og, o16, x, wts):
    B, T, _ = x.shape
    per = TILE // ROWS
    wo, wfg, wfu, wfd, gf, gl = wts
    tok = lambda c: pl.BlockSpec((None, ROWS, c), lambda b, i: (b, i, 0))
    slab = pl.BlockSpec((None, None, RES, SUB, SW), lambda b, i: (b, i // per, 0, i % per, 0))
    return pl.pallas_call(
        functools.partial(_ffn_body, os_slabs=True),
        grid=(B, T // ROWS),
        in_specs=[tok(GV), slab, tok(D_MODEL),
                  _resident(wo.shape), _resident(wfg.shape), _resident(wfu.shape),
                  _resident(wfd.shape), _const_spec(gf.shape), _const_spec(gl.shape)],
        out_specs=tok(D_MODEL),
        out_shape=jax.ShapeDtypeStruct((B, T, D_MODEL), F32),
        scratch_shapes=[pltpu.VMEM((SW // LANES, ROWS, LANES), F32),
                        pltpu.VMEM((SW // LANES, 4, ROWS // 4, LANES), F32)],
        compiler_params=_params(2),
        name="ffn_prompt",
    )(og, o16, x, wo, wfg, wfu, wfd, gf, gl)


def _ffn_sample(og, osw, x, wts):
    N = x.shape[0]
    wo, wfg, wfu, wfd, gf, gl = wts
    spec = lambda c: pl.BlockSpec((ROWS, c), lambda i: (i, 0))
    return pl.pallas_call(
        functools.partial(_ffn_body, os_slabs=False),
        grid=(N // ROWS,),
        in_specs=[spec(GV), spec(SW), spec(D_MODEL),
                  _resident(wo.shape), _resident(wfg.shape), _resident(wfu.shape),
                  _resident(wfd.shape), _const_spec(gf.shape), _const_spec(gl.shape)],
        out_specs=spec(D_MODEL),
        out_shape=jax.ShapeDtypeStruct((N, D_MODEL), F32),
        compiler_params=_params(1),
        name="ffn_sample",
    )(og, osw, x, wo, wfg, wfu, wfd, gf, gl)


def _layer(xp, xs, s0, kbuf, vbuf, w_in, w_gup, b_gate, g_mix, g_gla, w_out, g_ffn,
           w_fg, w_fu, w_fd, g_last):
    B, T, _ = xp.shape
    Bd, Ts, _ = xs.shape
    N = Bd * Ts
    assert T % TILE == 0 and T >= WIN_MAX and WIN_MAX % ROWS == 0
    assert CHUNK % Ts == 0 and N % ROWS == 0
    w, wg, bg, gm = _pack_weights(w_in, w_gup, b_gate, g_mix)
    wts = _ffn_weights(w_out, g_ffn, w_fg, w_fu, w_fd, g_last)
    xs2 = xs.reshape(N, D_MODEL)

    qg, kg, vg, r, la, qs, ks, vs = _inproj_sample(xs2, w, wg, bg, gm)
    ogs, ss = _gla_sample(qg, kg, vg, la, r, g_gla, s0, Ts)
    wb = kbuf.shape[1]
    feat = lambda a: jnp.transpose(a, (0, 2, 3, 1)).reshape(Bd, SW, wb)

    qg, kg, vg, r, la, q16, k16, v16, kt, vt, kk, vk = _inproj_prompt(xp, w, wg, bg, gm)
    og, sp, oss = _gla_prompt(qg, kg, vg, la, r, g_gla, (qs, ks, vs, feat(kbuf), feat(vbuf), Ts))
    o16 = _attn_prompt(q16, k16, v16, kt, vt)
    yp = _ffn_prompt(og, o16, xp, wts)
    ys = _ffn_sample(ogs, oss, xs2, wts)
    keep = lambda a: jnp.transpose(a.reshape(B, SWA_HEADS, SWA_HD, WIN_MAX), (0, 3, 1, 2))
    new = lambda a: a.reshape(Bd, Ts, SWA_HEADS, SWA_HD)
    return yp, ys.reshape(Bd, Ts, D_MODEL), sp, ss, keep(kk), keep(vk), new(ks), new(vs)


def kernel(x_prompt, x_sample, state_gla, cache_swa_k, cache_swa_v, w_in, w_gate_up, b_gate,
           g_mix_norm, g_gla_norm, w_out, g_ffn_norm, w_ffn_gate, w_ffn_up, w_ffn_down, g_final):
    depth = w_in.shape[0]
    assert depth == 1, "the final norm is fused into the single layer's FFN kernel"
    outs = _layer(x_prompt, x_sample, state_gla[0], cache_swa_k[0], cache_swa_v[0],
                  w_in[0], w_gate_up[0], b_gate[0], g_mix_norm[0], g_gla_norm[0], w_out[0],
                  g_ffn_norm[0], w_ffn_gate[0], w_ffn_up[0], w_ffn_down[0], g_final)
    yp, ys = outs[:2]
    return (yp, ys) + tuple(o[None] for o in outs[2:])
```

```python
import functools

import numpy as np
import jax
import jax.numpy as jnp
from jax import lax
from jax.experimental import pallas as pl
from jax.experimental.pallas import tpu as pltpu

F32 = jnp.float32
BF16 = jnp.bfloat16

D_MODEL = 1024
GLA_HEADS = 4
GLA_DK = 64
GLA_DV = 128
GLA_RANK = 16
GLA_TEMP = 16.0
SWA_HEADS = 8
SWA_HD = 64
DILATED_PATTERNS = ((128, 1), (512, 4), (2048, 16))
WIN_MAX = 2048
D_FF = 2816
EPS = 1e-6

GK = GLA_HEADS * GLA_DK
GV = GLA_HEADS * GLA_DV
SW = SWA_HEADS * SWA_HD
LANES = 128
TILE = 2048
RES = 16
BLK = 128
CHUNK = 64
ROWS = 512
SUB = ROWS // RES
NEG = -1e30
LOG2E = float(np.log2(np.e))
FF_CHUNK = 256
FFN_BLOCKS = 2
C_QG, C_KG, C_VG, C_R, C_QS, C_KS, C_VS, C_LR, C_END = 0, 256, 512, 1024, 1536, 2048, 2560, 3072, 3200
VMEM_LIMIT = 56 * 1024 * 1024


def _rms(x, g):
    ms = jnp.mean(x * x, axis=-1, keepdims=True)
    return x * lax.rsqrt(ms + EPS) * g


def _sigmoid(x):
    return 1.0 / (1.0 + jnp.exp(-x))


def _dot(a, b):
    return jnp.dot(a, b, preferred_element_type=F32)


def _dot_nt(a, b):
    return lax.dot_general(a, b, (((1,), (1,)), ((), ())), preferred_element_type=F32)


def _dot_tn(a, b):
    return lax.dot_general(a, b, (((0,), (0,)), ((), ())), preferred_element_type=F32)


def _const_spec(shape):
    nd = len(shape)
    return pl.BlockSpec(shape, lambda *_: (0,) * nd)


def _resident(shape):
    nd = len(shape)
    return pl.BlockSpec(shape, lambda *_: (0,) * nd, pipeline_mode=pl.Buffered(1))


def _params(n_axes):
    return pltpu.CompilerParams(dimension_semantics=("arbitrary",) * n_axes,
                                vmem_limit_bytes=VMEM_LIMIT)


def _inproj_body(x_ref, w_ref, wg_ref, bg_ref, gm_ref,
                 qg_ref, kg_ref, vg_ref, r_ref, la_ref, *rest, prompt, keep_first):
    xn = _rms(x_ref[...], gm_ref[...]).astype(BF16)

    def proj(lo, hi):
        return _dot(xn, w_ref[:, lo:hi])

    def gla_qk():
        qg_ref[...] = proj(C_QG, C_KG) * (GLA_DK ** -0.5)
        kg_ref[...] = proj(C_KG, C_VG)

    def gla_gate():
        lr = proj(C_LR, C_END)
        z = _dot(lr.astype(BF16), wg_ref[...]) + bg_ref[...]
        log_sig = -(jnp.maximum(-z, 0.0) + jnp.log1p(jnp.exp(-jnp.abs(z))))
        la_ref[...] = log_sig * (LOG2E / GLA_TEMP)

    def gla_rest():
        vg_ref[...] = proj(C_VG, C_R).astype(vg_ref.dtype)
        r_ref[...] = proj(C_R, C_QS)

    gla_gate()
    if not prompt:
        qs_ref, ks_ref, vs_ref = rest
        qs_ref[...] = proj(C_QS, C_KS) * (SWA_HD ** -0.5)
        ks_ref[...] = proj(C_KS, C_VS)
        vs_ref[...] = proj(C_VS, C_LR)
        gla_qk()
        gla_rest()
        return
    q16_ref, k16_ref, v16_ref, kt_ref, vt_ref, kk_ref, vk_ref, stage, stage4 = rest

    def to_slabs(val, ref, buf):
        for sl in range(SW // LANES):
            stage[buf, sl] = val[:, sl * LANES:(sl + 1) * LANES]
        for sl in range(SW // LANES):
            for r4 in range(4):
                stage4[buf, sl, r4] = stage[buf, sl, pl.ds(r4, ROWS // 4, stride=4), :]
            for r4 in range(4):
                for c in range(4):
                    ref[4 * c + r4, :, sl * LANES:(sl + 1) * LANES] = (
                        stage4[buf, sl, r4, pl.ds(c, SUB, stride=4), :].astype(ref.dtype))

    k = proj(C_KS, C_VS)
    kt_ref[...] = k.astype(BF16)
    v = proj(C_VS, C_LR)
    to_slabs(k, k16_ref, 0)
    vt_ref[...] = v.astype(BF16)
    q = proj(C_QS, C_KS) * (SWA_HD ** -0.5 * LOG2E)
    to_slabs(v, v16_ref, 1)
    gla_qk()
    to_slabs(q, q16_ref, 2)
    gla_rest()

    @pl.when(pl.program_id(1) >= keep_first)
    def _():
        for sl in range(SW // LANES):
            kk_ref[sl * LANES:(sl + 1) * LANES, :] = stage[0, sl].T
            vk_ref[sl * LANES:(sl + 1) * LANES, :] = stage[1, sl].T


def _pack_weights(w_in, w_gup, b_gate, g_mix):
    sizes = (GK, GK, GV, GV, GLA_RANK, SW, SW, SW)
    offs = np.concatenate([[0], np.cumsum(sizes)])
    qg, kg, vg, r, lr, qs, ks, vs = [w_in[:, offs[i]:offs[i + 1]] for i in range(8)]
    lr = jnp.pad(lr, ((0, 0), (0, LANES - GLA_RANK)))
    w = jnp.concatenate([qg, kg, vg, r, qs, ks, vs, lr], axis=1).astype(BF16)
    wg = jnp.pad(w_gup, ((0, LANES - GLA_RANK), (0, 0))).astype(BF16)
    return w, wg, b_gate.reshape(1, GK), g_mix.reshape(1, D_MODEL)


def _inproj_prompt(x, w, wg, bg, gm):
    B, T, _ = x.shape
    nt, per = T // TILE, TILE // ROWS
    n = T // ROWS
    keep_first = n - WIN_MAX // ROWS
    tok = lambda c: pl.BlockSpec((None, ROWS, c), lambda b, i: (b, i, 0))
    slab = pl.BlockSpec((None, None, RES, SUB, SW), lambda b, i: (b, i // per, 0, i % per, 0))
    keep = pl.BlockSpec((None, SW, ROWS), lambda b, i: (b, 0, jnp.maximum(i - keep_first, 0)))
    tshape = lambda c, dt: jax.ShapeDtypeStruct((B, T, c), dt)
    sshape = lambda dt: jax.ShapeDtypeStruct((B, nt, RES, BLK, SW), dt)
    kshape = jax.ShapeDtypeStruct((B, SW, WIN_MAX), F32)
    return pl.pallas_call(
        functools.partial(_inproj_body, prompt=True, keep_first=keep_first),
        grid=(B, n),
        in_specs=[tok(D_MODEL), _resident(w.shape), _const_spec(wg.shape),
                  _const_spec(bg.shape), _const_spec(gm.shape)],
        out_specs=(tok(GK), tok(GK), tok(GV), tok(GV), tok(GK),
                   slab, slab, slab, tok(SW), tok(SW), keep, keep),
        out_shape=(tshape(GK, F32), tshape(GK, F32), tshape(GV, BF16), tshape(GV, F32),
                   tshape(GK, F32), sshape(F32), sshape(BF16), sshape(BF16),
                   tshape(SW, BF16), tshape(SW, BF16), kshape, kshape),
        scratch_shapes=[pltpu.VMEM((3, SW // LANES, ROWS, LANES), F32),
                        pltpu.VMEM((3, SW // LANES, 4, ROWS // 4, LANES), F32)],
        compiler_params=_params(2),
        name="inproj_prompt",
    )(x, w, wg, bg, gm)


def _inproj_sample(x, w, wg, bg, gm):
    N = x.shape[0]
    spec = lambda c: pl.BlockSpec((ROWS, c), lambda i: (i, 0))
    shp = lambda c: jax.ShapeDtypeStruct((N, c), F32)
    return pl.pallas_call(
        functools.partial(_inproj_body, prompt=False, keep_first=0),
        grid=(N // ROWS,),
        in_specs=[spec(D_MODEL), _resident(w.shape), _const_spec(wg.shape),
                  _const_spec(bg.shape), _const_spec(gm.shape)],
        out_specs=(spec(GK), spec(GK), spec(GV), spec(GV), spec(GK), spec(SW), spec(SW), spec(SW)),
        out_shape=(shp(GK), shp(GK), shp(GV), shp(GV), shp(GK), shp(SW), shp(SW), shp(SW)),
        compiler_params=_params(1),
        name="inproj_sample",
    )(x, w, wg, bg, gm)


def _gla_masks(C, seg):
    ti = lax.broadcasted_iota(jnp.int32, (C, C), 0)
    si = lax.broadcasted_iota(jnp.int32, (C, C), 1)
    rows = lax.broadcasted_iota(jnp.int32, (C, GK), 0)
    masks, signs = [ti == si], [None]
    h, sh = 1, 0
    while h < seg:
        masks.append(jnp.logical_and(((ti ^ si) >> sh) == 1, si < ti))
        signs.append(jnp.where((rows & (2 * h - 1)) >= h, 1.0, -1.0).astype(F32))
        h, sh = 2 * h, sh + 1
    return masks, signs


def _gla_chunk(q, k, v, la, r, g, seg, states, consts):
    C = q.shape[0]
    nseg = C // seg
    rows = lax.broadcasted_iota(jnp.int32, (C, GK), 0)
    rin = rows & (seg - 1)
    b_inc = la
    d = 1
    while d < seg:
        b_inc = b_inc + jnp.where(rin >= d, pltpu.roll(b_inc, d, 0), 0.0)
        d *= 2

    lane = lax.broadcasted_iota(jnp.int32, (C, LANES), 1)
    head_lo = lane < GLA_DK
    row_c = lax.broadcasted_iota(jnp.int32, (C, LANES), 0)

    def pair(a, p):
        return a[:, p * LANES:(p + 1) * LANES]

    def by_head(a):
        return jnp.concatenate([jnp.where(head_lo, a, 0.0), jnp.where(head_lo, 0.0, a)], axis=0)

    A = [jnp.zeros((C, C), F32) for _ in range(GLA_HEADS)]

    def add_level(qs, ks, mask):
        for p in range(2):
            a2 = _dot_nt(by_head(pair(qs, p)).astype(BF16), pair(ks, p).astype(BF16))
            A[2 * p] = jnp.where(mask, a2[:C], A[2 * p])
            A[2 * p + 1] = jnp.where(mask, a2[C:], A[2 * p + 1])

    masks, signs = consts
    SUBL = 8
    last8 = jnp.broadcast_to(b_inc.reshape(C // SUBL, SUBL, GK)[:, SUBL - 1:, :],
                             (C // SUBL, SUBL, GK))

    def group_end(h):
        n = C // (2 * h)
        picked = last8.reshape(n, 2 * h // SUBL, SUBL, GK)[:, h // SUBL - 1:h // SUBL]
        return jnp.broadcast_to(picked, (n, 2 * h // SUBL, SUBL, GK)).reshape(C, GK)

    add_level(q, k, masks[0])
    end = b_inc
    h, lvl = 1, 1
    while h < seg:
        if h >= SUBL:
            bound = group_end(h)
        else:
            upper = (rin & (2 * h - 1)) >= h
            bound = jnp.where(upper, pltpu.roll(end, h, 0), end)
            end = jnp.where(upper, end, pltpu.roll(end, C - h, 0))
        fac = jnp.exp2((b_inc - bound) * signs[lvl])
        add_level(q * fac, k * fac, masks[lvl])
        h, lvl = 2 * h, lvl + 1
    if seg >= SUBL:
        end = jnp.broadcast_to(last8.reshape(nseg, seg // SUBL, SUBL, GK)[:, seg // SUBL - 1:],
                               (nseg, seg // SUBL, SUBL, GK)).reshape(C, GK)
    q_int = q * jnp.exp2(b_inc)
    k_upd = k * jnp.exp2(end - b_inc)

    def seg_rows(j):
        return jnp.logical_and(row_c >= j * seg, row_c < (j + 1) * seg)

    outs = []
    for p in range(2):
        qp = pair(q_int, p)
        inter = jnp.zeros((2 * C, LANES), F32)
        for j in range(nseg):
            qj = qp if nseg == 1 else jnp.where(seg_rows(j), qp, 0.0)
            inter = inter + _dot(by_head(qj).astype(BF16), states[j][p].astype(BF16))
        for hp in range(2):
            hd = 2 * p + hp
            vh = v[:, hd * GLA_DV:(hd + 1) * GLA_DV]
            o = _dot(A[hd].astype(BF16), vh) + inter[hp * C:(hp + 1) * C]
            gh = g[:, hd * GLA_DV:(hd + 1) * GLA_DV]
            rh = r[:, hd * GLA_DV:(hd + 1) * GLA_DV]
            outs.append(_rms(o, gh) * (rh * _sigmoid(rh)))

    r128 = lax.broadcasted_iota(jnp.int32, (LANES, LANES), 0)
    c128 = lax.broadcasted_iota(jnp.int32, (LANES, LANES), 1)
    new_states = []
    for j in range(nseg):
        st = []
        for p in range(2):
            kp = pair(k_upd, p)
            if nseg > 1:
                kp = jnp.where(seg_rows(j), kp, 0.0)
            kp = kp.astype(BF16)
            t0 = _dot_tn(kp, v[:, (2 * p) * GLA_DV:(2 * p + 1) * GLA_DV])
            t1 = _dot_tn(kp, v[:, (2 * p + 1) * GLA_DV:(2 * p + 2) * GLA_DV])
            t = jnp.where(r128 < GLA_DK, t0, t1)
            e_row = jnp.exp2(pair(end, p)[j * seg:j * seg + 1, :])
            e_col = jnp.sum(jnp.where(r128 == c128, jnp.broadcast_to(e_row, (LANES, LANES)), 0.0),
                            axis=1, keepdims=True)
            st.append(e_col * states[j][p] + t)
        new_states.append(st)
    return jnp.concatenate(outs, axis=1), new_states


N_RIDER_IN = 9


def _gla_prompt_body(q_ref, k_ref, v_ref, la_ref, r_ref, g_ref, *rest, n_chunks, rider_ts):
    rider_in, (o_ref, s_ref, so_ref, s_scr) = rest[:N_RIDER_IN], rest[N_RIDER_IN:]
    i = pl.program_id(1)

    @pl.when(i == 0)
    def _():
        s_scr[...] = jnp.zeros_like(s_scr)

    _attn_sample_body(*rider_in, so_ref, ts=rider_ts)
    g = g_ref[...]
    masks = _gla_masks(CHUNK, CHUNK)

    def step(c, carry):
        sl = pl.ds(pl.multiple_of(c * CHUNK, CHUNK), CHUNK)
        o, st = _gla_chunk(q_ref[sl, :], k_ref[sl, :], v_ref[sl, :], la_ref[sl, :], r_ref[sl, :],
                           g, CHUNK, [[s_scr[0], s_scr[1]]], masks)
        o_ref[sl, :] = o.astype(o_ref.dtype)
        s_scr[0] = st[0][0]
        s_scr[1] = st[0][1]
        return carry

    lax.fori_loop(0, n_chunks, step, 0, unroll=True)

    @pl.when(i == pl.num_programs(1) - 1)
    def _():
        s_ref[...] = s_scr[...]


def _gla_prompt(qg, kg, vg, la, r, g_gla, sample):
    B, T, _ = qg.shape
    n = T // ROWS
    assert sample[3].shape[0] == B * n, "one sample sequence per prompt GLA grid step"
    spec = lambda c: pl.BlockSpec((None, ROWS, c), lambda b, i: (b, i, 0))
    r_ops, r_specs, r_out_spec, r_out_shape = _attn_sample_operands(*sample, lambda b, i: b * n + i)
    assert len(r_ops) == N_RIDER_IN
    og, s, oss = pl.pallas_call(
        functools.partial(_gla_prompt_body, n_chunks=ROWS // CHUNK, rider_ts=sample[5]),
        grid=(B, n),
        in_specs=[spec(GK), spec(GK), spec(GV), spec(GK), spec(GV), _const_spec((1, GV))] + r_specs,
        out_specs=(spec(GV), pl.BlockSpec((None, 2, LANES, LANES), lambda b, i: (b, 0, 0, 0)),
                   r_out_spec),
        out_shape=(jax.ShapeDtypeStruct((B, T, GV), BF16),
                   jax.ShapeDtypeStruct((B, 2, LANES, LANES), F32), r_out_shape),
        scratch_shapes=[pltpu.VMEM((2, LANES, LANES), F32)],
        compiler_params=_params(2),
        name="gla_prompt",
    )(qg, kg, vg, la, r, g_gla.reshape(1, GV), *r_ops)
    return og, s.reshape(B, GLA_HEADS, GLA_DK, GLA_DV), oss


def _gla_sample_body(q_ref, k_ref, v_ref, la_ref, r_ref, g_ref, s0_ref, o_ref, s_ref, *, seg):
    nseg = CHUNK // seg
    states = [[s0_ref[j, 0], s0_ref[j, 1]] for j in range(nseg)]
    o, st = _gla_chunk(q_ref[...], k_ref[...], v_ref[...].astype(BF16), la_ref[...], r_ref[...],
                       g_ref[...], seg, states, _gla_masks(CHUNK, seg))
    o_ref[...] = o
    for j in range(nseg):
        s_ref[j, 0] = st[j][0]
        s_ref[j, 1] = st[j][1]


def _gla_sample(qg, kg, vg, la, r, g_gla, s0, seg):
    N = qg.shape[0]
    nb = CHUNK // seg
    nseq = N // seg
    spec = lambda c: pl.BlockSpec((CHUNK, c), lambda i: (i, 0))
    sspec = pl.BlockSpec((nb, 2, LANES, LANES), lambda i: (i, 0, 0, 0))
    og, s = pl.pallas_call(
        functools.partial(_gla_sample_body, seg=seg),
        grid=(N // CHUNK,),
        in_specs=[spec(GK), spec(GK), spec(GV), spec(GK), spec(GV), _const_spec((1, GV)), sspec],
        out_specs=(spec(GV), sspec),
        out_shape=(jax.ShapeDtypeStruct((N, GV), F32),
                   jax.ShapeDtypeStruct((nseq, 2, LANES, LANES), F32)),
        compiler_params=_params(1),
        name="gla_sample",
    )(qg, kg, vg, la, r, g_gla.reshape(1, GV), s0.reshape(nseq, 2, LANES, LANES))
    return og, s.reshape(nseq, GLA_HEADS, GLA_DK, GLA_DV)


def _alibi_slopes():
    return np.exp2(-8.0 * (np.arange(SWA_HEADS, dtype=np.float64) + 1.0) / SWA_HEADS)


def _prompt_bias_tables():
    slopes = _alibi_slopes()
    idx = np.arange(BLK)
    q_nat = {16: idx, 4: 4 * (idx % 32) + idx // 32, 1: 16 * (idx % 8) + idx // 8}
    k_nat = {16: idx, 4: 4 * (idx % 32) + idx // 32, 1: idx}
    out = np.empty((3, 2, SWA_HEADS, BLK, 2 * BLK), np.float32)
    for pi, d in enumerate((16, 4, 1)):
        j = np.concatenate([k_nat[d], BLK + k_nat[d]])[None, :]
        stp = q_nat[d][:, None] + BLK - j
        band = (stp >= 0) & (stp <= BLK)
        for first in (0, 1):
            valid = band & ((j >= BLK) if first else True)
            bias = -LOG2E * slopes[:, None, None] * (stp * d).astype(np.float64)[None]
            out[pi, first] = np.where(valid[None], bias, NEG).astype(np.float32)
    return out


def _attend(q, kk, vv, bias_of, prev):
    lo = lax.broadcasted_iota(jnp.int32, (BLK, LANES), 1) < SWA_HD
    lo_k = lax.broadcasted_iota(jnp.int32, (2 * BLK, LANES), 1) < SWA_HD
    zero, one = jnp.zeros_like(q), jnp.ones_like(vv)
    q2 = jnp.concatenate([jnp.where(lo, q, zero), jnp.where(lo, zero, q)], axis=0)
    s2 = _dot_nt(q2, kk)
    m_new, res, alphas = [], [], []
    for hp in range(2):
        s = s2[hp * BLK:(hp + 1) * BLK] + bias_of(hp)
        mc = jnp.max(s, axis=1, keepdims=True)
        if prev is None:
            mn = jnp.broadcast_to(mc, (BLK, LANES))
        else:
            mn = jnp.maximum(prev[0][hp], mc)
            alphas.append(jnp.exp2(prev[0][hp] - mn))
        p = jnp.exp2(s - jnp.concatenate([mn, mn], axis=1)).astype(BF16)
        w = jnp.where(lo_k, vv, one) if hp == 0 else jnp.where(lo_k, one, vv)
        res.append(_dot(p, w))
        m_new.append(mn)
    acc = jnp.where(lo, res[0], res[1])
    l = jnp.where(lo, res[1], res[0])
    if prev is not None:
        acc = jnp.where(lo, alphas[0], alphas[1]) * prev[2] + acc
        l = jnp.where(lo, alphas[1], alphas[0]) * prev[1] + l
    return m_new, l, acc


def _attn_prompt_body(q_ref, kc_ref, vc_ref, kp_ref, vp_ref, ktc_ref, vtc_ref, ktp_ref, vtp_ref,
                      bias_ref, o_ref, m_scr, l_scr, a_scr, tk_scr, tv_scr):
    tile0 = (pl.program_id(2) == 0).astype(jnp.int32)
    tk_scr[0:BLK, :] = ktp_ref[...]
    tk_scr[BLK:, :] = ktc_ref[...]
    tv_scr[0:BLK, :] = vtp_ref[...]
    tv_scr[BLK:, :] = vtc_ref[...]

    for r in range(RES):
        kk = jnp.concatenate([kp_ref[r], kc_ref[r]], axis=0)
        vv = jnp.concatenate([vp_ref[r], vc_ref[r]], axis=0)
        m, l, a = _attend(q_ref[r].astype(BF16), kk, vv, lambda hp: bias_ref[0, tile0, hp], None)
        for hp in range(2):
            m_scr[hp, r] = m[hp]
        l_scr[r] = l
        a_scr[r] = a

    for r4 in range(4):
        for kb in range(4):
            rows = slice(32 * kb, 32 * kb + 32)

            def keys(cur, prev):
                if kb == 0:
                    before = [prev[4 * c + r4, 96:128, :] for c in range(4)]
                else:
                    before = [cur[4 * c + r4, 32 * kb - 32:32 * kb, :] for c in range(4)]
                return jnp.concatenate(before + [cur[4 * c + r4, rows, :] for c in range(4)], axis=0)

            cat = lambda ref, *i: jnp.concatenate([ref[(*i, 4 * c + r4, rows)] for c in range(4)], axis=0)
            first = tile0 if kb == 0 else 0
            prev = ([cat(m_scr, hp) for hp in range(2)], cat(l_scr), cat(a_scr))
            m, l, a = _attend(cat(q_ref).astype(BF16), keys(kc_ref, kp_ref), keys(vc_ref, vp_ref),
                              lambda hp: bias_ref[1, first, hp], prev)
            for c in range(4):
                cs = slice(32 * c, 32 * c + 32)
                for hp in range(2):
                    m_scr[hp, 4 * c + r4, rows, :] = m[hp][cs]
                l_scr[4 * c + r4, rows, :] = l[cs]
                a_scr[4 * c + r4, rows, :] = a[cs]

    for j in range(RES):
        rs = slice(j * 8, j * 8 + 8)
        ks = slice(j * BLK, j * BLK + 2 * BLK)
        cat = lambda ref, *idx: ref[(*idx, slice(None), rs, slice(None))].reshape(BLK, LANES)
        first = tile0 if j == 0 else 0
        prev = ([cat(m_scr, hp) for hp in range(2)], cat(l_scr), cat(a_scr))
        m, l, a = _attend(cat(q_ref).astype(BF16), tk_scr[ks, :], tv_scr[ks, :],
                          lambda hp: bias_ref[2, first, hp], prev)
        for hp in range(2):
            m_scr[hp, :, rs, :] = m[hp].reshape(RES, 8, LANES)
        l_scr[:, rs, :] = l.reshape(RES, 8, LANES)
        a_scr[:, rs, :] = a.reshape(RES, 8, LANES)

    for r in range(RES):
        o_ref[r] = (a_scr[r] / pltpu.roll(l_scr[r], SWA_HD, 1)).astype(o_ref.dtype)


def _attn_prompt(q16, k16, v16, kt, vt):
    B, nt = q16.shape[0], q16.shape[1]
    bias = jnp.asarray(_prompt_bias_tables())
    tile = (None, None, RES, BLK, LANES)
    cur = pl.BlockSpec(tile, lambda g, b, a: (b, a, 0, 0, g))
    prv = pl.BlockSpec(tile, lambda g, b, a: (b, jnp.maximum(a - 1, 0), 0, 0, g))
    tcur = pl.BlockSpec((None, TILE, LANES), lambda g, b, a: (b, a, g))
    tprv = pl.BlockSpec((None, BLK, LANES),
                        lambda g, b, a: (b, jnp.maximum(a * (TILE // BLK) - 1, 0), g))
    bias_spec = pl.BlockSpec((3, 2, 2, BLK, 2 * BLK), lambda g, b, a: (0, 0, g, 0, 0))
    return pl.pallas_call(
        _attn_prompt_body,
        grid=(SWA_HEADS // 2, B, nt),
        in_specs=[cur, cur, cur, prv, prv, tcur, tcur, tprv, tprv, bias_spec],
        out_specs=cur,
        out_shape=jax.ShapeDtypeStruct((B, nt, RES, BLK, SW), BF16),
        scratch_shapes=[pltpu.VMEM((2, RES, BLK, LANES), F32),
                        pltpu.VMEM((RES, BLK, LANES), F32),
                        pltpu.VMEM((RES, BLK, LANES), F32),
                        pltpu.VMEM((TILE + BLK, LANES), BF16),
                        pltpu.VMEM((TILE + BLK, LANES), BF16)],
        compiler_params=_params(3),
        name="attn_prompt",
    )(q16, k16, v16, k16, v16, kt, vt, kt, vt, bias)


def _sample_tables(wb, ts):
    slopes = _alibi_slopes()
    ncol = wb + LANES
    t = np.arange(ts)[:, None]
    c = np.arange(ncol)[None, :]
    dist = wb + t - c
    mult = np.zeros((ts, ncol), np.float64)
    for (W, d) in DILATED_PATTERNS:
        mult += (dist >= 0) & (dist % d == 0) & (dist // d <= W // d) & (c < wb + ts)
    bias = np.where(mult[None] > 0, -slopes[:, None, None] * dist[None].astype(np.float64), NEG)
    mult = np.broadcast_to(mult[None], bias.shape)
    shp = (SWA_HEADS * ts, ncol)
    bias = bias.reshape(shp).astype(np.float32)
    mult = mult.reshape(shp).astype(np.float32)
    return bias[:, :wb], mult[:, :wb], bias[:, wb:], mult[:, wb:]


def _attn_sample_body(q_ref, kn_ref, vn_ref, kt_ref, vt_ref, bb_ref, mb_ref, bn_ref, mn_ref, o_ref,
                      *, ts):
    nr = SWA_HEADS * ts
    q = q_ref[...]
    qt = jnp.concatenate([q] * SWA_HEADS, axis=0)
    rh = lax.broadcasted_iota(jnp.int32, (nr, SW), 0) // ts
    ch = lax.broadcasted_iota(jnp.int32, (nr, SW), 1) // SWA_HD
    same = rh == ch
    qbd = jnp.where(same, qt, 0.0).astype(BF16)
    zpad = jnp.zeros((LANES - ts, SW), F32)
    kn = jnp.concatenate([kn_ref[...], zpad], axis=0).astype(BF16)
    vn = jnp.concatenate([vn_ref[...], zpad], axis=0).astype(BF16)
    s_b = _dot(qbd, kt_ref[...].astype(BF16)) + bb_ref[...]
    s_n = _dot_nt(qbd, kn) + bn_ref[...]
    m = jnp.maximum(jnp.max(s_b, axis=1, keepdims=True), jnp.max(s_n, axis=1, keepdims=True))
    p_b = jnp.exp(s_b - m) * mb_ref[...]
    p_n = jnp.exp(s_n - m) * mn_ref[...]
    den = jnp.sum(p_b, axis=1, keepdims=True) + jnp.sum(p_n, axis=1, keepdims=True)
    num = _dot_nt(p_b.astype(BF16), vt_ref[...].astype(BF16)) + _dot(p_n.astype(BF16), vn)
    full = jnp.where(same, num / den, 0.0)
    o = full[0:ts]
    for h in range(1, SWA_HEADS):
        o = o + full[h * ts:(h + 1) * ts]
    o_ref[...] = o


def _attn_sample_operands(qs, kn, vn, kbt, vbt, ts, seq_of):
    Bd, _, wb = kbt.shape
    for (W, d) in DILATED_PATTERNS:
        assert wb - (W // d) * d >= 0, "window buffer shorter than a pattern's reach"
    tables = [jnp.asarray(t) for t in _sample_tables(wb, ts)]
    row = pl.BlockSpec((ts, SW), lambda *ids: (seq_of(*ids), 0))
    buf = pl.BlockSpec((None, SW, wb), lambda *ids: (seq_of(*ids), 0, 0))
    operands = [qs, kn, vn, kbt, vbt] + tables
    specs = [row, row, row, buf, buf] + [_const_spec(t.shape) for t in tables]
    return operands, specs, row, jax.ShapeDtypeStruct((Bd * ts, SW), F32)


def _ffn_body(og_ref, os_ref, x_ref, wo_ref, wfg_ref, wfu_ref, wfd_ref, gf_ref, gl_ref, y_ref,
              *scratch, os_slabs):
    for blk in range(x_ref.shape[0] // ROWS):
        rows = slice(blk * ROWS, (blk + 1) * ROWS)
        if os_slabs:
            stage, stage4 = scratch
            sub = slice(blk * SUB, (blk + 1) * SUB)
            for sl in range(SW // LANES):
                for r4 in range(4):
                    for c in range(4):
                        stage4[blk, sl, r4, pl.ds(c, SUB, stride=4), :] = (
                            os_ref[4 * c + r4, sub, sl * LANES:(sl + 1) * LANES].astype(F32))
                for r4 in range(4):
                    stage[blk, sl, pl.ds(r4, ROWS // 4, stride=4), :] = stage4[blk, sl, r4]
            osw = jnp.concatenate([stage[blk, sl] for sl in range(SW // LANES)],
                                  axis=1).astype(BF16)
        else:
            osw = os_ref[rows, :].astype(BF16)
        og = og_ref[rows, :].astype(BF16)
        h = x_ref[rows, :] + _dot(og, wo_ref[0:GV, :]) + _dot(osw, wo_ref[GV:GV + SW, :])
        hn = _rms(h, gf_ref[...]).astype(BF16)
        acc = jnp.zeros(h.shape, F32)
        for c in range(D_FF // FF_CHUNK):
            cs = slice(c * FF_CHUNK, (c + 1) * FF_CHUNK)
            gate = _dot(hn, wfg_ref[:, cs])
            up = _dot(hn, wfu_ref[:, cs])
            act = (gate * _sigmoid(gate) * up).astype(BF16)
            acc = acc + _dot(act, wfd_ref[cs, :])
        y_ref[rows, :] = _rms(h + acc, gl_ref[...])


def _ffn_weights(w_out, g_ffn, w_fg, w_fu, w_fd, g_final):
    return (w_out.astype(BF16), w_fg.astype(BF16), w_fu.astype(BF16), w_fd.astype(BF16),
            g_ffn.reshape(1, D_MODEL), g_final.reshape(1, D_MODEL))


def _ffn_prompt(og, o16, x, wts):
    B, T, _ = x.shape
    nblk = FFN_BLOCKS
    step = nblk * ROWS
    per = TILE // step
    wo, wfg, wfu, wfd, gf, gl = wts
    tok = lambda c: pl.BlockSpec((None, step, c), lambda b, i: (b, i, 0))
    slab = pl.BlockSpec((None, None, RES, nblk * SUB, SW),
                        lambda b, i: (b, i // per, 0, i % per, 0))
    return pl.pallas_call(
        functools.partial(_ffn_body, os_slabs=True),
        grid=(B, T // step),
        in_specs=[tok(GV), slab, tok(D_MODEL),
                  _resident(wo.shape), _resident(wfg.shape), _resident(wfu.shape),
                  _resident(wfd.shape), _const_spec(gf.shape), _const_spec(gl.shape)],
        out_specs=tok(D_MODEL),
        out_shape=jax.ShapeDtypeStruct((B, T, D_MODEL), F32),
        scratch_shapes=[pltpu.VMEM((nblk, SW // LANES, ROWS, LANES), F32),
                        pltpu.VMEM((nblk, SW // LANES, 4, ROWS // 4, LANES), F32)],
        compiler_params=_params(2),
        name="ffn_prompt",
    )(og, o16, x, wo, wfg, wfu, wfd, gf, gl)


def _ffn_sample(og, osw, x, wts):
    N = x.shape[0]
    wo, wfg, wfu, wfd, gf, gl = wts
    spec = lambda c: pl.BlockSpec((ROWS, c), lambda i: (i, 0))
    return pl.pallas_call(
        functools.partial(_ffn_body, os_slabs=False),
        grid=(N // ROWS,),
        in_specs=[spec(GV), spec(SW), spec(D_MODEL),
                  _resident(wo.shape), _resident(wfg.shape), _resident(wfu.shape),
                  _resident(wfd.shape), _const_spec(gf.shape), _const_spec(gl.shape)],
        out_specs=spec(D_MODEL),
        out_shape=jax.ShapeDtypeStruct((N, D_MODEL), F32),
        compiler_params=_params(1),
        name="ffn_sample",
    )(og, osw, x, wo, wfg, wfu, wfd, gf, gl)


def _layer(xp, xs, s0, kbuf, vbuf, w_in, w_gup, b_gate, g_mix, g_gla, w_out, g_ffn,
           w_fg, w_fu, w_fd, g_last):
    B, T, _ = xp.shape
    Bd, Ts, _ = xs.shape
    N = Bd * Ts
    assert T % TILE == 0 and T >= WIN_MAX and WIN_MAX % ROWS == 0
    assert CHUNK % Ts == 0 and N % ROWS == 0
    w, wg, bg, gm = _pack_weights(w_in, w_gup, b_gate, g_mix)
    wts = _ffn_weights(w_out, g_ffn, w_fg, w_fu, w_fd, g_last)
    xs2 = xs.reshape(N, D_MODEL)

    qg, kg, vg, r, la, qs, ks, vs = _inproj_sample(xs2, w, wg, bg, gm)
    ogs, ss = _gla_sample(qg, kg, vg, la, r, g_gla, s0, Ts)
    wb = kbuf.shape[1]
    feat = lambda a: jnp.transpose(a, (0, 2, 3, 1)).reshape(Bd, SW, wb)

    qg, kg, vg, r, la, q16, k16, v16, kt, vt, kk, vk = _inproj_prompt(xp, w, wg, bg, gm)
    og, sp, oss = _gla_prompt(qg, kg, vg, la, r, g_gla, (qs, ks, vs, feat(kbuf), feat(vbuf), Ts))
    o16 = _attn_prompt(q16, k16, v16, kt, vt)
    yp = _ffn_prompt(og, o16, xp, wts)
    ys = _ffn_sample(ogs, oss, xs2, wts)
    keep = lambda a: jnp.transpose(a.reshape(B, SWA_HEADS, SWA_HD, WIN_MAX), (0, 3, 1, 2))
    new = lambda a: a.reshape(Bd, Ts, SWA_HEADS, SWA_HD)
    return yp, ys.reshape(Bd, Ts, D_MODEL), sp, ss, keep(kk), keep(vk), new(ks), new(vs)


def kernel(x_prompt, x_sample, state_gla, cache_swa_k, cache_swa_v, w_in, w_gate_up, b_gate,
           g_mix_norm, g_gla_norm, w_out, g_ffn_norm, w_ffn_gate, w_ffn_up, w_ffn_down, g_final):
    depth = w_in.shape[0]
    assert depth == 1, "the final norm is fused into the single layer's FFN kernel"
    outs = _layer(x_prompt, x_sample, state_gla[0], cache_swa_k[0], cache_swa_v[0],
                  w_in[0], w_gate_up[0], b_gate[0], g_mix_norm[0], g_gla_norm[0], w_out[0],
                  g_ffn_norm[0], w_ffn_gate[0], w_ffn_up[0], w_ffn_down[0], g_final)
    yp, ys = outs[:2]
    return (yp, ys) + tuple(o[None] for o in outs[2:])
```

```python
import functools

import numpy as np
import jax
import jax.numpy as jnp
from jax import lax
from jax.experimental import pallas as pl
from jax.experimental.pallas import tpu as pltpu

F32 = jnp.float32
BF16 = jnp.bfloat16

D_MODEL = 1024
GLA_HEADS = 4
GLA_DK = 64
GLA_DV = 128
GLA_RANK = 16
GLA_TEMP = 16.0
SWA_HEADS = 8
SWA_HD = 64
DILATED_PATTERNS = ((128, 1), (512, 4), (2048, 16))
WIN_MAX = 2048
D_FF = 2816
EPS = 1e-6

GK = GLA_HEADS * GLA_DK
GV = GLA_HEADS * GLA_DV
SW = SWA_HEADS * SWA_HD
LANES = 128
SUBLANES = 8
TILE = 2048
RES = 16
BLK = 128
CHUNK = 64
ROWS = 512
SUB = ROWS // RES
NEG = -1e30
LOG2E = float(np.log2(np.e))
FF_CHUNK = 256
FFN_BLOCKS = 2
C_QG, C_KG, C_VG, C_R, C_QS, C_KS, C_VS, C_LR, C_END = 0, 256, 512, 1024, 1536, 2048, 2560, 3072, 3200
VMEM_LIMIT = 56 * 1024 * 1024


def _rms(x, g):
    ms = jnp.mean(x * x, axis=-1, keepdims=True)
    return x * lax.rsqrt(ms + EPS) * g


def _sigmoid(x):
    return 1.0 / (1.0 + jnp.exp(-x))


def _dot(a, b):
    return jnp.dot(a, b, preferred_element_type=F32)


def _dot_nt(a, b):
    return lax.dot_general(a, b, (((1,), (1,)), ((), ())), preferred_element_type=F32)


def _dot_tn(a, b):
    return lax.dot_general(a, b, (((0,), (0,)), ((), ())), preferred_element_type=F32)


def _const_spec(shape):
    nd = len(shape)
    return pl.BlockSpec(shape, lambda *_: (0,) * nd)


def _resident(shape):
    nd = len(shape)
    return pl.BlockSpec(shape, lambda *_: (0,) * nd, pipeline_mode=pl.Buffered(1))


def _params(n_axes):
    return pltpu.CompilerParams(dimension_semantics=("arbitrary",) * n_axes,
                                vmem_limit_bytes=VMEM_LIMIT)


def _inproj_body(x_ref, w_ref, wg_ref, bg_ref, gm_ref,
                 qg_ref, kg_ref, vg_ref, r_ref, la_ref, *rest, prompt):
    xn = _rms(x_ref[...], gm_ref[...]).astype(BF16)

    def proj(lo, hi):
        return _dot(xn, w_ref[:, lo:hi])

    def gla_qk():
        qg_ref[...] = proj(C_QG, C_KG) * (GLA_DK ** -0.5)
        kg_ref[...] = proj(C_KG, C_VG)

    def gla_gate():
        lr = proj(C_LR, C_END)
        z = _dot(lr.astype(BF16), wg_ref[...]) + bg_ref[...]
        log_sig = -(jnp.maximum(-z, 0.0) + jnp.log1p(jnp.exp(-jnp.abs(z))))
        la_ref[...] = log_sig * (LOG2E / GLA_TEMP)

    def gla_rest():
        vg_ref[...] = proj(C_VG, C_R).astype(vg_ref.dtype)
        r_ref[...] = proj(C_R, C_QS)

    gla_gate()
    if not prompt:
        qs_ref, ks_ref, vs_ref = rest
        qs_ref[...] = proj(C_QS, C_KS) * (SWA_HD ** -0.5)
        ks_ref[...] = proj(C_KS, C_VS)
        vs_ref[...] = proj(C_VS, C_LR)
        gla_qk()
        gla_rest()
        return
    q16_ref, k16_ref, v16_ref, kt_ref, vt_ref, kk_ref, vk_ref, stage, stage4 = rest

    def to_slabs(val, ref, buf):
        for sl in range(SW // LANES):
            stage[buf, sl] = val[:, sl * LANES:(sl + 1) * LANES]
        for sl in range(SW // LANES):
            for r4 in range(4):
                stage4[buf, sl, r4] = stage[buf, sl, pl.ds(r4, ROWS // 4, stride=4), :]
            for r4 in range(4):
                for c in range(4):
                    ref[4 * c + r4, :, sl * LANES:(sl + 1) * LANES] = (
                        stage4[buf, sl, r4, pl.ds(c, SUB, stride=4), :].astype(ref.dtype))

    k = proj(C_KS, C_VS)
    kt_ref[...] = k.astype(BF16)
    v = proj(C_VS, C_LR)
    to_slabs(k, k16_ref, 0)
    vt_ref[...] = v.astype(BF16)
    q = proj(C_QS, C_KS) * (SWA_HD ** -0.5 * LOG2E)
    to_slabs(v, v16_ref, 1)
    gla_qk()
    to_slabs(q, q16_ref, 2)
    gla_rest()

    for sl in range(SW // LANES):
        kk_ref[sl * LANES:(sl + 1) * LANES, :] = stage[0, sl].T
        vk_ref[sl * LANES:(sl + 1) * LANES, :] = stage[1, sl].T


def _pack_weights(w_in, w_gup, b_gate, g_mix):
    sizes = (GK, GK, GV, GV, GLA_RANK, SW, SW, SW)
    offs = np.concatenate([[0], np.cumsum(sizes)])
    qg, kg, vg, r, lr, qs, ks, vs = [w_in[:, offs[i]:offs[i + 1]] for i in range(8)]
    lr = jnp.pad(lr, ((0, 0), (0, LANES - GLA_RANK)))
    w = jnp.concatenate([qg, kg, vg, r, qs, ks, vs, lr], axis=1).astype(BF16)
    wg = jnp.pad(w_gup, ((0, LANES - GLA_RANK), (0, 0))).astype(BF16)
    return w, wg, b_gate.reshape(1, GK), g_mix.reshape(1, D_MODEL)


def _inproj_prompt(x, w, wg, bg, gm):
    B, T, _ = x.shape
    nt, per = T // TILE, TILE // ROWS
    n = T // ROWS
    keep_first = n - WIN_MAX // ROWS
    tok = lambda c: pl.BlockSpec((None, ROWS, c), lambda b, i: (b, i, 0))
    slab = pl.BlockSpec((None, None, RES, SUB, SW), lambda b, i: (b, i // per, 0, i % per, 0))
    keep = pl.BlockSpec((None, SW, ROWS), lambda b, i: (b, 0, jnp.maximum(i - keep_first, 0)))
    tshape = lambda c, dt: jax.ShapeDtypeStruct((B, T, c), dt)
    sshape = lambda dt: jax.ShapeDtypeStruct((B, nt, RES, BLK, SW), dt)
    kshape = jax.ShapeDtypeStruct((B, SW, WIN_MAX), F32)
    return pl.pallas_call(
        functools.partial(_inproj_body, prompt=True),
        grid=(B, n),
        in_specs=[tok(D_MODEL), _resident(w.shape), _const_spec(wg.shape),
                  _const_spec(bg.shape), _const_spec(gm.shape)],
        out_specs=(tok(GK), tok(GK), tok(GV), tok(GV), tok(GK),
                   slab, slab, slab, tok(SW), tok(SW), keep, keep),
        out_shape=(tshape(GK, F32), tshape(GK, F32), tshape(GV, BF16), tshape(GV, F32),
                   tshape(GK, F32), sshape(F32), sshape(BF16), sshape(BF16),
                   tshape(SW, BF16), tshape(SW, BF16), kshape, kshape),
        scratch_shapes=[pltpu.VMEM((3, SW // LANES, ROWS, LANES), F32),
                        pltpu.VMEM((3, SW // LANES, 4, ROWS // 4, LANES), F32)],
        compiler_params=_params(2),
        name="inproj_prompt",
    )(x, w, wg, bg, gm)


def _inproj_sample(x, w, wg, bg, gm):
    N = x.shape[0]
    spec = lambda c: pl.BlockSpec((ROWS, c), lambda i: (i, 0))
    shp = lambda c: jax.ShapeDtypeStruct((N, c), F32)
    return pl.pallas_call(
        functools.partial(_inproj_body, prompt=False),
        grid=(N // ROWS,),
        in_specs=[spec(D_MODEL), _resident(w.shape), _const_spec(wg.shape),
                  _const_spec(bg.shape), _const_spec(gm.shape)],
        out_specs=(spec(GK), spec(GK), spec(GV), spec(GV), spec(GK), spec(SW), spec(SW), spec(SW)),
        out_shape=(shp(GK), shp(GK), shp(GV), shp(GV), shp(GK), shp(SW), shp(SW), shp(SW)),
        compiler_params=_params(1),
        name="inproj_sample",
    )(x, w, wg, bg, gm)


def _gla_masks(C, seg):
    ti = lax.broadcasted_iota(jnp.int32, (C, C), 0)
    si = lax.broadcasted_iota(jnp.int32, (C, C), 1)
    rows = lax.broadcasted_iota(jnp.int32, (C, GK), 0)
    masks, signs = [ti == si], [None]
    h, sh = 1, 0
    while h < seg:
        masks.append(jnp.logical_and(((ti ^ si) >> sh) == 1, si < ti))
        signs.append(jnp.where((rows & (2 * h - 1)) >= h, 1.0, -1.0).astype(F32))
        h, sh = 2 * h, sh + 1
    return masks, signs


def _gla_chunk(q, k, v, la, r, g, seg, states, consts):
    C = q.shape[0]
    nseg = C // seg
    rows = lax.broadcasted_iota(jnp.int32, (C, GK), 0)
    rin = rows & (seg - 1)
    b_inc = la
    d = 1
    while d < seg:
        b_inc = b_inc + jnp.where(rin >= d, pltpu.roll(b_inc, d, 0), 0.0)
        d *= 2

    lane = lax.broadcasted_iota(jnp.int32, (C, LANES), 1)
    head_lo = lane < GLA_DK
    row_c = lax.broadcasted_iota(jnp.int32, (C, LANES), 0)

    def pair(a, p):
        return a[:, p * LANES:(p + 1) * LANES]

    def by_head(a):
        return jnp.concatenate([jnp.where(head_lo, a, 0.0), jnp.where(head_lo, 0.0, a)], axis=0)

    A = [jnp.zeros((C, C), F32) for _ in range(GLA_HEADS)]

    def add_level(qs, ks, mask):
        for p in range(2):
            a2 = _dot_nt(by_head(pair(qs, p)).astype(BF16), pair(ks, p).astype(BF16))
            A[2 * p] = jnp.where(mask, a2[:C], A[2 * p])
            A[2 * p + 1] = jnp.where(mask, a2[C:], A[2 * p + 1])

    masks, signs = consts
    SUBL = SUBLANES
    last8 = jnp.broadcast_to(b_inc.reshape(C // SUBL, SUBL, GK)[:, SUBL - 1:, :],
                             (C // SUBL, SUBL, GK))

    def group_end(h):
        n = C // (2 * h)
        picked = last8.reshape(n, 2 * h // SUBL, SUBL, GK)[:, h // SUBL - 1:h // SUBL]
        return jnp.broadcast_to(picked, (n, 2 * h // SUBL, SUBL, GK)).reshape(C, GK)

    add_level(q, k, masks[0])
    end = b_inc
    h, lvl = 1, 1
    while h < seg:
        if h >= SUBL:
            bound = group_end(h)
        else:
            upper = (rin & (2 * h - 1)) >= h
            bound = jnp.where(upper, pltpu.roll(end, h, 0), end)
            end = jnp.where(upper, end, pltpu.roll(end, C - h, 0))
        fac = jnp.exp2((b_inc - bound) * signs[lvl])
        add_level(q * fac, k * fac, masks[lvl])
        h, lvl = 2 * h, lvl + 1
    if seg >= SUBL:
        end = jnp.broadcast_to(last8.reshape(nseg, seg // SUBL, SUBL, GK)[:, seg // SUBL - 1:],
                               (nseg, seg // SUBL, SUBL, GK)).reshape(C, GK)
    q_int = q * jnp.exp2(b_inc)
    k_upd = k * jnp.exp2(end - b_inc)

    def seg_rows(j):
        return jnp.logical_and(row_c >= j * seg, row_c < (j + 1) * seg)

    outs = []
    for p in range(2):
        qp = pair(q_int, p)
        inter = jnp.zeros((2 * C, LANES), F32)
        for j in range(nseg):
            qj = qp if nseg == 1 else jnp.where(seg_rows(j), qp, 0.0)
            inter = inter + _dot(by_head(qj).astype(BF16), states[j][p].astype(BF16))
        for hp in range(2):
            hd = 2 * p + hp
            vh = v[:, hd * GLA_DV:(hd + 1) * GLA_DV]
            o = _dot(A[hd].astype(BF16), vh) + inter[hp * C:(hp + 1) * C]
            gh = g[:, hd * GLA_DV:(hd + 1) * GLA_DV]
            rh = r[:, hd * GLA_DV:(hd + 1) * GLA_DV]
            outs.append(_rms(o, gh) * (rh * _sigmoid(rh)))

    r128 = lax.broadcasted_iota(jnp.int32, (LANES, LANES), 0)
    c128 = lax.broadcasted_iota(jnp.int32, (LANES, LANES), 1)
    new_states = []
    for j in range(nseg):
        st = []
        for p in range(2):
            kp = pair(k_upd, p)
            if nseg > 1:
                kp = jnp.where(seg_rows(j), kp, 0.0)
            kp = kp.astype(BF16)
            t0 = _dot_tn(kp, v[:, (2 * p) * GLA_DV:(2 * p + 1) * GLA_DV])
            t1 = _dot_tn(kp, v[:, (2 * p + 1) * GLA_DV:(2 * p + 2) * GLA_DV])
            t = jnp.where(r128 < GLA_DK, t0, t1)
            e_row = jnp.exp2(pair(end, p)[j * seg:j * seg + 1, :])
            e_col = jnp.sum(jnp.where(r128 == c128, jnp.broadcast_to(e_row, (LANES, LANES)), 0.0),
                            axis=1, keepdims=True)
            st.append(e_col * states[j][p] + t)
        new_states.append(st)
    return jnp.concatenate(outs, axis=1), new_states


N_RIDER_IN = 9


def _gla_prompt_body(q_ref, k_ref, v_ref, la_ref, r_ref, g_ref, *rest, n_chunks, rider_ts):
    rider_in, (o_ref, s_ref, so_ref, s_scr) = rest[:N_RIDER_IN], rest[N_RIDER_IN:]
    i = pl.program_id(1)

    @pl.when(i == 0)
    def _():
        s_scr[...] = jnp.zeros_like(s_scr)

    _attn_sample_body(*rider_in, so_ref, ts=rider_ts)
    g = g_ref[...]
    masks = _gla_masks(CHUNK, CHUNK)

    def step(c, carry):
        sl = pl.ds(pl.multiple_of(c * CHUNK, CHUNK), CHUNK)
        o, st = _gla_chunk(q_ref[sl, :], k_ref[sl, :], v_ref[sl, :], la_ref[sl, :], r_ref[sl, :],
                           g, CHUNK, [[s_scr[0], s_scr[1]]], masks)
        o_ref[sl, :] = o.astype(o_ref.dtype)
        s_scr[0] = st[0][0]
        s_scr[1] = st[0][1]
        return carry

    lax.fori_loop(0, n_chunks, step, 0, unroll=True)

    @pl.when(i == pl.num_programs(1) - 1)
    def _():
        s_ref[...] = s_scr[...]


def _gla_prompt(qg, kg, vg, la, r, g_gla, sample):
    B, T, _ = qg.shape
    n = T // ROWS
    assert sample[3].shape[0] == B * n, "one sample sequence per prompt GLA grid step"
    spec = lambda c: pl.BlockSpec((None, ROWS, c), lambda b, i: (b, i, 0))
    r_ops, r_specs, r_out_spec, r_out_shape = _attn_sample_operands(*sample, lambda b, i: b * n + i)
    assert len(r_ops) == N_RIDER_IN
    og, s, oss = pl.pallas_call(
        functools.partial(_gla_prompt_body, n_chunks=ROWS // CHUNK, rider_ts=sample[5]),
        grid=(B, n),
        in_specs=[spec(GK), spec(GK), spec(GV), spec(GK), spec(GV), _const_spec((1, GV))] + r_specs,
        out_specs=(spec(GV), pl.BlockSpec((None, 2, LANES, LANES), lambda b, i: (b, 0, 0, 0)),
                   r_out_spec),
        out_shape=(jax.ShapeDtypeStruct((B, T, GV), BF16),
                   jax.ShapeDtypeStruct((B, 2, LANES, LANES), F32), r_out_shape),
        scratch_shapes=[pltpu.VMEM((2, LANES, LANES), F32)],
        compiler_params=_params(2),
        name="gla_prompt",
    )(qg, kg, vg, la, r, g_gla.reshape(1, GV), *r_ops)
    return og, s.reshape(B, GLA_HEADS, GLA_DK, GLA_DV), oss


def _gla_sample_body(q_ref, k_ref, v_ref, la_ref, r_ref, g_ref, s0_ref, o_ref, s_ref, *, seg):
    nseg = CHUNK // seg
    states = [[s0_ref[j, 0], s0_ref[j, 1]] for j in range(nseg)]
    o, st = _gla_chunk(q_ref[...], k_ref[...], v_ref[...].astype(BF16), la_ref[...], r_ref[...],
                       g_ref[...], seg, states, _gla_masks(CHUNK, seg))
    o_ref[...] = o
    for j in range(nseg):
        s_ref[j, 0] = st[j][0]
        s_ref[j, 1] = st[j][1]


def _gla_sample(qg, kg, vg, la, r, g_gla, s0, seg):
    N = qg.shape[0]
    nb = CHUNK // seg
    nseq = N // seg
    spec = lambda c: pl.BlockSpec((CHUNK, c), lambda i: (i, 0))
    sspec = pl.BlockSpec((nb, 2, LANES, LANES), lambda i: (i, 0, 0, 0))
    og, s = pl.pallas_call(
        functools.partial(_gla_sample_body, seg=seg),
        grid=(N // CHUNK,),
        in_specs=[spec(GK), spec(GK), spec(GV), spec(GK), spec(GV), _const_spec((1, GV)), sspec],
        out_specs=(spec(GV), sspec),
        out_shape=(jax.ShapeDtypeStruct((N, GV), F32),
                   jax.ShapeDtypeStruct((nseq, 2, LANES, LANES), F32)),
        compiler_params=_params(1),
        name="gla_sample",
    )(qg, kg, vg, la, r, g_gla.reshape(1, GV), s0.reshape(nseq, 2, LANES, LANES))
    return og, s.reshape(nseq, GLA_HEADS, GLA_DK, GLA_DV)


def _alibi_slopes():
    return np.exp2(-8.0 * (np.arange(SWA_HEADS, dtype=np.float64) + 1.0) / SWA_HEADS)


def _prompt_bias_tables():
    slopes = _alibi_slopes()
    idx = np.arange(BLK)
    q_nat = {16: idx, 4: 4 * (idx % 32) + idx // 32, 1: 16 * (idx % 8) + idx // 8}
    k_nat = {16: idx, 4: 4 * (idx % 32) + idx // 32, 1: idx}
    out = np.empty((3, 2, SWA_HEADS, BLK, 2 * BLK), np.float32)
    for pi, d in enumerate((16, 4, 1)):
        j = np.concatenate([k_nat[d], BLK + k_nat[d]])[None, :]
        stp = q_nat[d][:, None] + BLK - j
        band = (stp >= 0) & (stp <= BLK)
        for first in (0, 1):
            valid = band & ((j >= BLK) if first else True)
            bias = -LOG2E * slopes[:, None, None] * (stp * d).astype(np.float64)[None]
            out[pi, first] = np.where(valid[None], bias, NEG).astype(np.float32)
    return out


def _attend(q, kk, vv, bias_of, prev):
    lo = lax.broadcasted_iota(jnp.int32, (BLK, LANES), 1) < SWA_HD
    lo_k = lax.broadcasted_iota(jnp.int32, (2 * BLK, LANES), 1) < SWA_HD
    zero, one = jnp.zeros_like(q), jnp.ones_like(vv)
    q2 = jnp.concatenate([jnp.where(lo, q, zero), jnp.where(lo, zero, q)], axis=0)
    s2 = _dot_nt(q2, kk)
    m_new, res, alphas = [], [], []
    for hp in range(2):
        s = s2[hp * BLK:(hp + 1) * BLK] + bias_of(hp)
        mc = jnp.max(s, axis=1, keepdims=True)
        if prev is None:
            mn = jnp.broadcast_to(mc, (BLK, LANES))
        else:
            mn = jnp.maximum(prev[0][hp], mc)
            alphas.append(jnp.exp2(prev[0][hp] - mn))
        p = jnp.exp2(s - jnp.concatenate([mn, mn], axis=1)).astype(BF16)
        w = jnp.where(lo_k, vv, one) if hp == 0 else jnp.where(lo_k, one, vv)
        res.append(_dot(p, w))
        m_new.append(mn)
    acc = jnp.where(lo, res[0], res[1])
    l = jnp.where(lo, res[1], res[0])
    if prev is not None:
        acc = jnp.where(lo, alphas[0], alphas[1]) * prev[2] + acc
        l = jnp.where(lo, alphas[1], alphas[0]) * prev[1] + l
    return m_new, l, acc


def _attn_prompt_body(q_ref, kc_ref, vc_ref, kp_ref, vp_ref, ktc_ref, vtc_ref, ktp_ref, vtp_ref,
                      bias_ref, o_ref, m_scr, l_scr, a_scr, tk_scr, tv_scr):
    tile0 = (pl.program_id(2) == 0).astype(jnp.int32)
    tk_scr[0:BLK, :] = ktp_ref[...]
    tk_scr[BLK:, :] = ktc_ref[...]
    tv_scr[0:BLK, :] = vtp_ref[...]
    tv_scr[BLK:, :] = vtc_ref[...]

    for r in range(RES):
        kk = jnp.concatenate([kp_ref[r], kc_ref[r]], axis=0)
        vv = jnp.concatenate([vp_ref[r], vc_ref[r]], axis=0)
        m, l, a = _attend(q_ref[r].astype(BF16), kk, vv, lambda hp: bias_ref[0, tile0, hp], None)
        for hp in range(2):
            m_scr[hp, r] = m[hp]
        l_scr[r] = l
        a_scr[r] = a

    for r4 in range(4):
        for kb in range(4):
            qr = BLK // 4
            rows = slice(qr * kb, qr * kb + qr)

            def keys(cur, prev):
                if kb == 0:
                    before = [prev[4 * c + r4, BLK - qr:BLK, :] for c in range(4)]
                else:
                    before = [cur[4 * c + r4, qr * kb - qr:qr * kb, :] for c in range(4)]
                return jnp.concatenate(before + [cur[4 * c + r4, rows, :] for c in range(4)], axis=0)

            cat = lambda ref, *i: jnp.concatenate([ref[(*i, 4 * c + r4, rows)] for c in range(4)], axis=0)
            first = tile0 if kb == 0 else 0
            prev = ([cat(m_scr, hp) for hp in range(2)], cat(l_scr), cat(a_scr))
            m, l, a = _attend(cat(q_ref).astype(BF16), keys(kc_ref, kp_ref), keys(vc_ref, vp_ref),
                              lambda hp: bias_ref[1, first, hp], prev)
            for c in range(4):
                cs = slice(qr * c, qr * c + qr)
                for hp in range(2):
                    m_scr[hp, 4 * c + r4, rows, :] = m[hp][cs]
                l_scr[4 * c + r4, rows, :] = l[cs]
                a_scr[4 * c + r4, rows, :] = a[cs]

    for j in range(RES):
        rs = slice(j * SUBLANES, (j + 1) * SUBLANES)
        ks = slice(j * BLK, j * BLK + 2 * BLK)
        cat = lambda ref, *idx: ref[(*idx, slice(None), rs, slice(None))].reshape(BLK, LANES)
        first = tile0 if j == 0 else 0
        prev = ([cat(m_scr, hp) for hp in range(2)], cat(l_scr), cat(a_scr))
        m, l, a = _attend(cat(q_ref).astype(BF16), tk_scr[ks, :], tv_scr[ks, :],
                          lambda hp: bias_ref[2, first, hp], prev)
        for hp in range(2):
            m_scr[hp, :, rs, :] = m[hp].reshape(RES, SUBLANES, LANES)
        l_scr[:, rs, :] = l.reshape(RES, SUBLANES, LANES)
        a_scr[:, rs, :] = a.reshape(RES, SUBLANES, LANES)

    for r in range(RES):
        o_ref[r] = (a_scr[r] / pltpu.roll(l_scr[r], SWA_HD, 1)).astype(o_ref.dtype)


def _attn_prompt(q16, k16, v16, kt, vt):
    B, nt = q16.shape[0], q16.shape[1]
    bias = jnp.asarray(_prompt_bias_tables())
    tile = (None, None, RES, BLK, LANES)
    cur = pl.BlockSpec(tile, lambda g, b, a: (b, a, 0, 0, g))
    prv = pl.BlockSpec(tile, lambda g, b, a: (b, jnp.maximum(a - 1, 0), 0, 0, g))
    tcur = pl.BlockSpec((None, TILE, LANES), lambda g, b, a: (b, a, g))
    tprv = pl.BlockSpec((None, BLK, LANES),
                        lambda g, b, a: (b, jnp.maximum(a * (TILE // BLK) - 1, 0), g))
    bias_spec = pl.BlockSpec((3, 2, 2, BLK, 2 * BLK), lambda g, b, a: (0, 0, g, 0, 0))
    return pl.pallas_call(
        _attn_prompt_body,
        grid=(SWA_HEADS // 2, B, nt),
        in_specs=[cur, cur, cur, prv, prv, tcur, tcur, tprv, tprv, bias_spec],
        out_specs=cur,
        out_shape=jax.ShapeDtypeStruct((B, nt, RES, BLK, SW), BF16),
        scratch_shapes=[pltpu.VMEM((2, RES, BLK, LANES), F32),
                        pltpu.VMEM((RES, BLK, LANES), F32),
                        pltpu.VMEM((RES, BLK, LANES), F32),
                        pltpu.VMEM((TILE + BLK, LANES), BF16),
                        pltpu.VMEM((TILE + BLK, LANES), BF16)],
        compiler_params=_params(3),
        name="attn_prompt",
    )(q16, k16, v16, k16, v16, kt, vt, kt, vt, bias)


def _sample_tables(wb, ts):
    slopes = _alibi_slopes()
    ncol = wb + LANES
    t = np.arange(ts)[:, None]
    c = np.arange(ncol)[None, :]
    dist = wb + t - c
    mult = np.zeros((ts, ncol), np.float64)
    for (W, d) in DILATED_PATTERNS:
        mult += (dist >= 0) & (dist % d == 0) & (dist // d <= W // d) & (c < wb + ts)
    bias = np.where(mult[None] > 0, -slopes[:, None, None] * dist[None].astype(np.float64), NEG)
    mult = np.broadcast_to(mult[None], bias.shape)
    shp = (SWA_HEADS * ts, ncol)
    bias = bias.reshape(shp).astype(np.float32)
    mult = mult.reshape(shp).astype(np.float32)
    return bias[:, :wb], mult[:, :wb], bias[:, wb:], mult[:, wb:]


def _attn_sample_body(q_ref, kn_ref, vn_ref, kt_ref, vt_ref, bb_ref, mb_ref, bn_ref, mn_ref, o_ref,
                      *, ts):
    nr = SWA_HEADS * ts
    q = q_ref[...]
    qt = jnp.concatenate([q] * SWA_HEADS, axis=0)
    rh = lax.broadcasted_iota(jnp.int32, (nr, SW), 0) // ts
    ch = lax.broadcasted_iota(jnp.int32, (nr, SW), 1) // SWA_HD
    same = rh == ch
    qbd = jnp.where(same, qt, 0.0).astype(BF16)
    zpad = jnp.zeros((LANES - ts, SW), F32)
    kn = jnp.concatenate([kn_ref[...], zpad], axis=0).astype(BF16)
    vn = jnp.concatenate([vn_ref[...], zpad], axis=0).astype(BF16)
    s_b = _dot(qbd, kt_ref[...].astype(BF16)) + bb_ref[...]
    s_n = _dot_nt(qbd, kn) + bn_ref[...]
    m = jnp.maximum(jnp.max(s_b, axis=1, keepdims=True), jnp.max(s_n, axis=1, keepdims=True))
    p_b = jnp.exp(s_b - m) * mb_ref[...]
    p_n = jnp.exp(s_n - m) * mn_ref[...]
    den = jnp.sum(p_b, axis=1, keepdims=True) + jnp.sum(p_n, axis=1, keepdims=True)
    num = _dot_nt(p_b.astype(BF16), vt_ref[...].astype(BF16)) + _dot(p_n.astype(BF16), vn)
    full = jnp.where(same, num / den, 0.0)
    o = full[0:ts]
    for h in range(1, SWA_HEADS):
        o = o + full[h * ts:(h + 1) * ts]
    o_ref[...] = o


def _attn_sample_operands(qs, kn, vn, kbt, vbt, ts, seq_of):
    Bd, _, wb = kbt.shape
    for (W, d) in DILATED_PATTERNS:
        assert wb - (W // d) * d >= 0, "window buffer shorter than a pattern's reach"
    tables = [jnp.asarray(t) for t in _sample_tables(wb, ts)]
    row = pl.BlockSpec((ts, SW), lambda *ids: (seq_of(*ids), 0))
    buf = pl.BlockSpec((None, SW, wb), lambda *ids: (seq_of(*ids), 0, 0))
    operands = [qs, kn, vn, kbt, vbt] + tables
    specs = [row, row, row, buf, buf] + [_const_spec(t.shape) for t in tables]
    return operands, specs, row, jax.ShapeDtypeStruct((Bd * ts, SW), F32)


def _ffn_body(og_ref, os_ref, x_ref, wo_ref, wfg_ref, wfu_ref, wfd_ref, gf_ref, gl_ref, y_ref,
              *scratch, os_slabs):
    for blk in range(x_ref.shape[0] // ROWS):
        rows = slice(blk * ROWS, (blk + 1) * ROWS)
        if os_slabs:
            stage, stage4 = scratch
            sub = slice(blk * SUB, (blk + 1) * SUB)
            for sl in range(SW // LANES):
                for r4 in range(4):
                    for c in range(4):
                        stage4[blk, sl, r4, pl.ds(c, SUB, stride=4), :] = (
                            os_ref[4 * c + r4, sub, sl * LANES:(sl + 1) * LANES].astype(F32))
                for r4 in range(4):
                    stage[blk, sl, pl.ds(r4, ROWS // 4, stride=4), :] = stage4[blk, sl, r4]
            osw = jnp.concatenate([stage[blk, sl] for sl in range(SW // LANES)],
                                  axis=1).astype(BF16)
        else:
            osw = os_ref[rows, :].astype(BF16)
        og = og_ref[rows, :].astype(BF16)
        h = x_ref[rows, :] + _dot(og, wo_ref[0:GV, :]) + _dot(osw, wo_ref[GV:GV + SW, :])
        hn = _rms(h, gf_ref[...]).astype(BF16)
        acc = jnp.zeros(h.shape, F32)
        for c in range(D_FF // FF_CHUNK):
            cs = slice(c * FF_CHUNK, (c + 1) * FF_CHUNK)
            gate = _dot(hn, wfg_ref[:, cs])
            up = _dot(hn, wfu_ref[:, cs])
            act = (gate * _sigmoid(gate) * up).astype(BF16)
            acc = acc + _dot(act, wfd_ref[cs, :])
        y_ref[rows, :] = _rms(h + acc, gl_ref[...])


def _ffn_weights(w_out, g_ffn, w_fg, w_fu, w_fd, g_final):
    return (w_out.astype(BF16), w_fg.astype(BF16), w_fu.astype(BF16), w_fd.astype(BF16),
            g_ffn.reshape(1, D_MODEL), g_final.reshape(1, D_MODEL))


def _ffn_prompt(og, o16, x, wts):
    B, T, _ = x.shape
    nblk = FFN_BLOCKS
    step = nblk * ROWS
    per = TILE // step
    wo, wfg, wfu, wfd, gf, gl = wts
    tok = lambda c: pl.BlockSpec((None, step, c), lambda b, i: (b, i, 0))
    slab = pl.BlockSpec((None, None, RES, nblk * SUB, SW),
                        lambda b, i: (b, i // per, 0, i % per, 0))
    return pl.pallas_call(
        functools.partial(_ffn_body, os_slabs=True),
        grid=(B, T // step),
        in_specs=[tok(GV), slab, tok(D_MODEL),
                  _resident(wo.shape), _resident(wfg.shape), _resident(wfu.shape),
                  _resident(wfd.shape), _const_spec(gf.shape), _const_spec(gl.shape)],
        out_specs=tok(D_MODEL),
        out_shape=jax.ShapeDtypeStruct((B, T, D_MODEL), F32),
        scratch_shapes=[pltpu.VMEM((nblk, SW // LANES, ROWS, LANES), F32),
                        pltpu.VMEM((nblk, SW // LANES, 4, ROWS // 4, LANES), F32)],
        compiler_params=_params(2),
        name="ffn_prompt",
    )(og, o16, x, wo, wfg, wfu, wfd, gf, gl)


def _ffn_sample(og, osw, x, wts):
    N = x.shape[0]
    wo, wfg, wfu, wfd, gf, gl = wts
    spec = lambda c: pl.BlockSpec((ROWS, c), lambda i: (i, 0))
    return pl.pallas_call(
        functools.partial(_ffn_body, os_slabs=False),
        grid=(N // ROWS,),
        in_specs=[spec(GV), spec(SW), spec(D_MODEL),
                  _resident(wo.shape), _resident(wfg.shape), _resident(wfu.shape),
                  _resident(wfd.shape), _const_spec(gf.shape), _const_spec(gl.shape)],
        out_specs=spec(D_MODEL),
        out_shape=jax.ShapeDtypeStruct((N, D_MODEL), F32),
        compiler_params=_params(1),
        name="ffn_sample",
    )(og, osw, x, wo, wfg, wfu, wfd, gf, gl)


def _layer(xp, xs, s0, kbuf, vbuf, w_in, w_gup, b_gate, g_mix, g_gla, w_out, g_ffn,
           w_fg, w_fu, w_fd, g_last):
    B, T, _ = xp.shape
    Bd, Ts, _ = xs.shape
    N = Bd * Ts
    assert T % TILE == 0 and T >= WIN_MAX and WIN_MAX % ROWS == 0
    assert CHUNK % Ts == 0 and N % ROWS == 0
    w, wg, bg, gm = _pack_weights(w_in, w_gup, b_gate, g_mix)
    wts = _ffn_weights(w_out, g_ffn, w_fg, w_fu, w_fd, g_last)
    xs2 = xs.reshape(N, D_MODEL)

    qg, kg, vg, r, la, qs, ks, vs = _inproj_sample(xs2, w, wg, bg, gm)
    ogs, ss = _gla_sample(qg, kg, vg, la, r, g_gla, s0, Ts)
    wb = kbuf.shape[1]
    feat = lambda a: jnp.transpose(a, (0, 2, 3, 1)).reshape(Bd, SW, wb)

    qg, kg, vg, r, la, q16, k16, v16, kt, vt, kk, vk = _inproj_prompt(xp, w, wg, bg, gm)
    og, sp, oss = _gla_prompt(qg, kg, vg, la, r, g_gla, (qs, ks, vs, feat(kbuf), feat(vbuf), Ts))
    o16 = _attn_prompt(q16, k16, v16, kt, vt)
    yp = _ffn_prompt(og, o16, xp, wts)
    ys = _ffn_sample(ogs, oss, xs2, wts)
    keep = lambda a: jnp.transpose(a.reshape(B, SWA_HEADS, SWA_HD, WIN_MAX), (0, 3, 1, 2))
    new = lambda a: a.reshape(Bd, Ts, SWA_HEADS, SWA_HD)
    return yp, ys.reshape(Bd, Ts, D_MODEL), sp, ss, keep(kk), keep(vk), new(ks), new(vs)


def kernel(x_prompt, x_sample, state_gla, cache_swa_k, cache_swa_v, w_in, w_gate_up, b_gate,
           g_mix_norm, g_gla_norm, w_out, g_ffn_norm, w_ffn_gate, w_ffn_up, w_ffn_down, g_final):
    depth = w_in.shape[0]
    assert depth == 1, "the final norm is fused into the single layer's FFN kernel"
    outs = _layer(x_prompt, x_sample, state_gla[0], cache_swa_k[0], cache_swa_v[0],
                  w_in[0], w_gate_up[0], b_gate[0], g_mix_norm[0], g_gla_norm[0], w_out[0],
                  g_ffn_norm[0], w_ffn_gate[0], w_ffn_up[0], w_ffn_down[0], g_final)
    yp, ys = outs[:2]
    return (yp, ys) + tuple(o[None] for o in outs[2:])
```

```python
import functools

import numpy as np
import jax
import jax.numpy as jnp
from jax import lax
from jax.experimental import pallas as pl
from jax.experimental.pallas import tpu as pltpu

F32 = jnp.float32
BF16 = jnp.bfloat16

D_MODEL = 1024
GLA_HEADS = 4
GLA_DK = 64
GLA_DV = 128
GLA_RANK = 16
GLA_TEMP = 16.0
SWA_HEADS = 8
SWA_HD = 64
DILATED_PATTERNS = ((128, 1), (512, 4), (2048, 16))
WIN_MAX = 2048
D_FF = 2816
EPS = 1e-6

GK = GLA_HEADS * GLA_DK
GV = GLA_HEADS * GLA_DV
SW = SWA_HEADS * SWA_HD
LANES = 128
SUBLANES = 8
TILE = 2048
RES = 16
BLK = 128
CHUNK = 64
ROWS = 512
SUB = ROWS // RES
NEG = -1e30
LOG2E = float(np.log2(np.e))
FF_CHUNK = 256
FFN_BLOCKS = 2
C_QG, C_KG, C_VG, C_R, C_QS, C_KS, C_VS, C_LR, C_END = 0, 256, 512, 1024, 1536, 2048, 2560, 3072, 3200
VMEM_LIMIT = 56 * 1024 * 1024


def _rms(x, g):
    ms = jnp.mean(x * x, axis=-1, keepdims=True)
    return x * lax.rsqrt(ms + EPS) * g


def _sigmoid(x):
    return 1.0 / (1.0 + jnp.exp(-x))


def _dot(a, b):
    return jnp.dot(a, b, preferred_element_type=F32)


def _dot_nt(a, b):
    return lax.dot_general(a, b, (((1,), (1,)), ((), ())), preferred_element_type=F32)


def _dot_tn(a, b):
    return lax.dot_general(a, b, (((0,), (0,)), ((), ())), preferred_element_type=F32)


def _const_spec(shape):
    nd = len(shape)
    return pl.BlockSpec(shape, lambda *_: (0,) * nd)


def _resident(shape):
    nd = len(shape)
    return pl.BlockSpec(shape, lambda *_: (0,) * nd, pipeline_mode=pl.Buffered(1))


def _params(n_axes):
    return pltpu.CompilerParams(dimension_semantics=("arbitrary",) * n_axes,
                                vmem_limit_bytes=VMEM_LIMIT)


def _inproj_body(x_ref, w_ref, wg_ref, bg_ref, gm_ref,
                 qg_ref, kg_ref, vg_ref, r_ref, la_ref, *rest, prompt):
    xn = _rms(x_ref[...], gm_ref[...]).astype(BF16)

    def proj(lo, hi):
        return _dot(xn, w_ref[:, lo:hi])

    def gla_qk():
        qg_ref[...] = proj(C_QG, C_KG) * (GLA_DK ** -0.5)
        kg_ref[...] = proj(C_KG, C_VG)

    def gla_gate():
        lr = proj(C_LR, C_END)
        z = _dot(lr.astype(BF16), wg_ref[...]) + bg_ref[...]
        log_sig = -(jnp.maximum(-z, 0.0) + jnp.log1p(jnp.exp(-jnp.abs(z))))
        la_ref[...] = log_sig * (LOG2E / GLA_TEMP)

    def gla_rest():
        vg_ref[...] = proj(C_VG, C_R).astype(vg_ref.dtype)
        r_ref[...] = proj(C_R, C_QS)

    gla_gate()
    if not prompt:
        qs_ref, ks_ref, vs_ref = rest
        qs_ref[...] = proj(C_QS, C_KS) * (SWA_HD ** -0.5)
        ks_ref[...] = proj(C_KS, C_VS)
        vs_ref[...] = proj(C_VS, C_LR)
        gla_qk()
        gla_rest()
        return
    q16_ref, k16_ref, v16_ref, kt_ref, vt_ref, kk_ref, vk_ref, stage, stage4 = rest

    def to_slabs(val, ref, buf):
        for sl in range(SW // LANES):
            stage[buf, sl] = val[:, sl * LANES:(sl + 1) * LANES]
        for sl in range(SW // LANES):
            for r4 in range(4):
                stage4[buf, sl, r4] = stage[buf, sl, pl.ds(r4, ROWS // 4, stride=4), :]
            for r4 in range(4):
                for c in range(4):
                    ref[4 * c + r4, :, sl * LANES:(sl + 1) * LANES] = (
                        stage4[buf, sl, r4, pl.ds(c, SUB, stride=4), :].astype(ref.dtype))

    k = proj(C_KS, C_VS)
    kt_ref[...] = k.astype(BF16)
    v = proj(C_VS, C_LR)
    to_slabs(k, k16_ref, 0)
    vt_ref[...] = v.astype(BF16)
    q = proj(C_QS, C_KS) * (SWA_HD ** -0.5 * LOG2E)
    to_slabs(v, v16_ref, 1)
    gla_qk()
    to_slabs(q, q16_ref, 2)
    gla_rest()

    for sl in range(SW // LANES):
        kk_ref[sl * LANES:(sl + 1) * LANES, :] = stage[0, sl].T
        vk_ref[sl * LANES:(sl + 1) * LANES, :] = stage[1, sl].T


def _pack_weights(w_in, w_gup, b_gate, g_mix):
    sizes = (GK, GK, GV, GV, GLA_RANK, SW, SW, SW)
    offs = np.concatenate([[0], np.cumsum(sizes)])
    qg, kg, vg, r, lr, qs, ks, vs = [w_in[:, offs[i]:offs[i + 1]] for i in range(8)]
    lr = jnp.pad(lr, ((0, 0), (0, LANES - GLA_RANK)))
    w = jnp.concatenate([qg, kg, vg, r, qs, ks, vs, lr], axis=1).astype(BF16)
    wg = jnp.pad(w_gup, ((0, LANES - GLA_RANK), (0, 0))).astype(BF16)
    return w, wg, b_gate.reshape(1, GK), g_mix.reshape(1, D_MODEL)


def _inproj_prompt(x, w, wg, bg, gm):
    B, T, _ = x.shape
    nt, per = T // TILE, TILE // ROWS
    n = T // ROWS
    keep_first = n - WIN_MAX // ROWS
    tok = lambda c: pl.BlockSpec((None, ROWS, c), lambda b, i: (b, i, 0))
    slab = pl.BlockSpec((None, None, RES, SUB, SW), lambda b, i: (b, i // per, 0, i % per, 0))
    keep = pl.BlockSpec((None, SW, ROWS), lambda b, i: (b, 0, jnp.maximum(i - keep_first, 0)))
    tshape = lambda c, dt: jax.ShapeDtypeStruct((B, T, c), dt)
    sshape = lambda dt: jax.ShapeDtypeStruct((B, nt, RES, BLK, SW), dt)
    kshape = jax.ShapeDtypeStruct((B, SW, WIN_MAX), F32)
    return pl.pallas_call(
        functools.partial(_inproj_body, prompt=True),
        grid=(B, n),
        in_specs=[tok(D_MODEL), _resident(w.shape), _const_spec(wg.shape),
                  _const_spec(bg.shape), _const_spec(gm.shape)],
        out_specs=(tok(GK), tok(GK), tok(GV), tok(GV), tok(GK),
                   slab, slab, slab, tok(SW), tok(SW), keep, keep),
        out_shape=(tshape(GK, F32), tshape(GK, F32), tshape(GV, BF16), tshape(GV, F32),
                   tshape(GK, F32), sshape(F32), sshape(BF16), sshape(BF16),
                   tshape(SW, BF16), tshape(SW, BF16), kshape, kshape),
        scratch_shapes=[pltpu.VMEM((3, SW // LANES, ROWS, LANES), F32),
                        pltpu.VMEM((3, SW // LANES, 4, ROWS // 4, LANES), F32)],
        compiler_params=_params(2),
        name="inproj_prompt",
    )(x, w, wg, bg, gm)


def _inproj_sample(x, w, wg, bg, gm):
    N = x.shape[0]
    spec = lambda c: pl.BlockSpec((ROWS, c), lambda i: (i, 0))
    shp = lambda c: jax.ShapeDtypeStruct((N, c), F32)
    return pl.pallas_call(
        functools.partial(_inproj_body, prompt=False),
        grid=(N // ROWS,),
        in_specs=[spec(D_MODEL), _resident(w.shape), _const_spec(wg.shape),
                  _const_spec(bg.shape), _const_spec(gm.shape)],
        out_specs=(spec(GK), spec(GK), spec(GV), spec(GV), spec(GK), spec(SW), spec(SW), spec(SW)),
        out_shape=(shp(GK), shp(GK), shp(GV), shp(GV), shp(GK), shp(SW), shp(SW), shp(SW)),
        compiler_params=_params(1),
        name="inproj_sample",
    )(x, w, wg, bg, gm)


def _gla_masks(C, seg):
    ti = lax.broadcasted_iota(jnp.int32, (C, C), 0)
    si = lax.broadcasted_iota(jnp.int32, (C, C), 1)
    rows = lax.broadcasted_iota(jnp.int32, (C, GK), 0)
    masks, signs = [ti == si], [None]
    h, sh = 1, 0
    while h < seg:
        masks.append(jnp.logical_and(((ti ^ si) >> sh) == 1, si < ti))
        signs.append(jnp.where((rows & (2 * h - 1)) >= h, 1.0, -1.0).astype(F32))
        h, sh = 2 * h, sh + 1
    return masks, signs


def _gla_chunk(q, k, v, la, r, g, seg, states, consts):
    C = q.shape[0]
    nseg = C // seg
    rows = lax.broadcasted_iota(jnp.int32, (C, GK), 0)
    rin = rows & (seg - 1)
    b_inc = la
    d = 1
    while d < seg:
        b_inc = b_inc + jnp.where(rin >= d, pltpu.roll(b_inc, d, 0), 0.0)
        d *= 2

    lane = lax.broadcasted_iota(jnp.int32, (C, LANES), 1)
    head_lo = lane < GLA_DK
    row_c = lax.broadcasted_iota(jnp.int32, (C, LANES), 0)

    def pair(a, p):
        return a[:, p * LANES:(p + 1) * LANES]

    def by_head(a):
        return jnp.concatenate([jnp.where(head_lo, a, 0.0), jnp.where(head_lo, 0.0, a)], axis=0)

    A = [jnp.zeros((C, C), F32) for _ in range(GLA_HEADS)]

    def add_level(qs, ks, mask):
        for p in range(2):
            a2 = _dot_nt(by_head(pair(qs, p)).astype(BF16), pair(ks, p).astype(BF16))
            A[2 * p] = jnp.where(mask, a2[:C], A[2 * p])
            A[2 * p + 1] = jnp.where(mask, a2[C:], A[2 * p + 1])

    masks, signs = consts
    SUBL = SUBLANES
    last8 = jnp.broadcast_to(b_inc.reshape(C // SUBL, SUBL, GK)[:, SUBL - 1:, :],
                             (C // SUBL, SUBL, GK))

    def group_end(h):
        n = C // (2 * h)
        picked = last8.reshape(n, 2 * h // SUBL, SUBL, GK)[:, h // SUBL - 1:h // SUBL]
        return jnp.broadcast_to(picked, (n, 2 * h // SUBL, SUBL, GK)).reshape(C, GK)

    add_level(q, k, masks[0])
    end = b_inc
    h, lvl = 1, 1
    while h < seg:
        if h >= SUBL:
            bound = group_end(h)
        else:
            upper = (rin & (2 * h - 1)) >= h
            bound = jnp.where(upper, pltpu.roll(end, h, 0), end)
            end = jnp.where(upper, end, pltpu.roll(end, C - h, 0))
        fac = jnp.exp2((b_inc - bound) * signs[lvl])
        add_level(q * fac, k * fac, masks[lvl])
        h, lvl = 2 * h, lvl + 1
    if seg >= SUBL:
        end = jnp.broadcast_to(last8.reshape(nseg, seg // SUBL, SUBL, GK)[:, seg // SUBL - 1:],
                               (nseg, seg // SUBL, SUBL, GK)).reshape(C, GK)
    q_int = q * jnp.exp2(b_inc)
    k_upd = k * jnp.exp2(end - b_inc)

    def seg_rows(j):
        return jnp.logical_and(row_c >= j * seg, row_c < (j + 1) * seg)

    outs = []
    for p in range(2):
        qp = pair(q_int, p)
        inter = jnp.zeros((2 * C, LANES), F32)
        for j in range(nseg):
            qj = qp if nseg == 1 else jnp.where(seg_rows(j), qp, 0.0)
            inter = inter + _dot(by_head(qj).astype(BF16), states[j][p].astype(BF16))
        for hp in range(2):
            hd = 2 * p + hp
            vh = v[:, hd * GLA_DV:(hd + 1) * GLA_DV]
            o = _dot(A[hd].astype(BF16), vh) + inter[hp * C:(hp + 1) * C]
            gh = g[:, hd * GLA_DV:(hd + 1) * GLA_DV]
            rh = r[:, hd * GLA_DV:(hd + 1) * GLA_DV]
            outs.append(_rms(o, gh) * (rh * _sigmoid(rh)))

    r128 = lax.broadcasted_iota(jnp.int32, (LANES, LANES), 0)
    c128 = lax.broadcasted_iota(jnp.int32, (LANES, LANES), 1)
    new_states = []
    for j in range(nseg):
        st = []
        for p in range(2):
            kp = pair(k_upd, p)
            if nseg > 1:
                kp = jnp.where(seg_rows(j), kp, 0.0)
            kp = kp.astype(BF16)
            t0 = _dot_tn(kp, v[:, (2 * p) * GLA_DV:(2 * p + 1) * GLA_DV])
            t1 = _dot_tn(kp, v[:, (2 * p + 1) * GLA_DV:(2 * p + 2) * GLA_DV])
            t = jnp.where(r128 < GLA_DK, t0, t1)
            e_row = jnp.exp2(pair(end, p)[j * seg:j * seg + 1, :])
            e_col = jnp.sum(jnp.where(r128 == c128, jnp.broadcast_to(e_row, (LANES, LANES)), 0.0),
                            axis=1, keepdims=True)
            st.append(e_col * states[j][p] + t)
        new_states.append(st)
    return jnp.concatenate(outs, axis=1), new_states


N_RIDER_IN = 9


def _gla_prompt_body(q_ref, k_ref, v_ref, la_ref, r_ref, g_ref, *rest, n_chunks, rider_ts):
    rider_in, (o_ref, s_ref, so_ref) = rest[:N_RIDER_IN], rest[N_RIDER_IN:]
    s_scr = s_ref

    @pl.when(pl.program_id(1) == 0)
    def _():
        s_scr[...] = jnp.zeros_like(s_scr)

    _attn_sample_body(*rider_in, so_ref, ts=rider_ts)
    g = g_ref[...]
    masks = _gla_masks(CHUNK, CHUNK)

    def step(c, carry):
        sl = pl.ds(pl.multiple_of(c * CHUNK, CHUNK), CHUNK)
        o, st = _gla_chunk(q_ref[sl, :], k_ref[sl, :], v_ref[sl, :], la_ref[sl, :], r_ref[sl, :],
                           g, CHUNK, [[s_scr[0], s_scr[1]]], masks)
        o_ref[sl, :] = o.astype(o_ref.dtype)
        s_scr[0] = st[0][0]
        s_scr[1] = st[0][1]
        return carry

    lax.fori_loop(0, n_chunks, step, 0, unroll=True)


def _gla_prompt(qg, kg, vg, la, r, g_gla, sample):
    B, T, _ = qg.shape
    n = T // ROWS
    assert sample[3].shape[0] == B * n, "one sample sequence per prompt GLA grid step"
    spec = lambda c: pl.BlockSpec((None, ROWS, c), lambda b, i: (b, i, 0))
    r_ops, r_specs, r_out_spec, r_out_shape = _attn_sample_operands(*sample, lambda b, i: b * n + i)
    assert len(r_ops) == N_RIDER_IN
    og, s, oss = pl.pallas_call(
        functools.partial(_gla_prompt_body, n_chunks=ROWS // CHUNK, rider_ts=sample[5]),
        grid=(B, n),
        in_specs=[spec(GK), spec(GK), spec(GV), spec(GK), spec(GV), _const_spec((1, GV))] + r_specs,
        out_specs=(spec(GV), pl.BlockSpec((None, 2, LANES, LANES), lambda b, i: (b, 0, 0, 0)),
                   r_out_spec),
        out_shape=(jax.ShapeDtypeStruct((B, T, GV), BF16),
                   jax.ShapeDtypeStruct((B, 2, LANES, LANES), F32), r_out_shape),
        compiler_params=_params(2),
        name="gla_prompt",
    )(qg, kg, vg, la, r, g_gla.reshape(1, GV), *r_ops)
    return og, s.reshape(B, GLA_HEADS, GLA_DK, GLA_DV), oss


def _gla_sample_body(q_ref, k_ref, v_ref, la_ref, r_ref, g_ref, s0_ref, o_ref, s_ref, *, seg):
    nseg = CHUNK // seg
    states = [[s0_ref[j, 0], s0_ref[j, 1]] for j in range(nseg)]
    o, st = _gla_chunk(q_ref[...], k_ref[...], v_ref[...].astype(BF16), la_ref[...], r_ref[...],
                       g_ref[...], seg, states, _gla_masks(CHUNK, seg))
    o_ref[...] = o
    for j in range(nseg):
        s_ref[j, 0] = st[j][0]
        s_ref[j, 1] = st[j][1]


def _gla_sample(qg, kg, vg, la, r, g_gla, s0, seg):
    N = qg.shape[0]
    nb = CHUNK // seg
    nseq = N // seg
    spec = lambda c: pl.BlockSpec((CHUNK, c), lambda i: (i, 0))
    sspec = pl.BlockSpec((nb, 2, LANES, LANES), lambda i: (i, 0, 0, 0))
    og, s = pl.pallas_call(
        functools.partial(_gla_sample_body, seg=seg),
        grid=(N // CHUNK,),
        in_specs=[spec(GK), spec(GK), spec(GV), spec(GK), spec(GV), _const_spec((1, GV)), sspec],
        out_specs=(spec(GV), sspec),
        out_shape=(jax.ShapeDtypeStruct((N, GV), F32),
                   jax.ShapeDtypeStruct((nseq, 2, LANES, LANES), F32)),
        compiler_params=_params(1),
        name="gla_sample",
    )(qg, kg, vg, la, r, g_gla.reshape(1, GV), s0.reshape(nseq, 2, LANES, LANES))
    return og, s.reshape(nseq, GLA_HEADS, GLA_DK, GLA_DV)


def _alibi_slopes():
    return np.exp2(-8.0 * (np.arange(SWA_HEADS, dtype=np.float64) + 1.0) / SWA_HEADS)


def _prompt_bias_tables():
    slopes = _alibi_slopes()
    idx = np.arange(BLK)
    q_nat = {16: idx, 4: 4 * (idx % 32) + idx // 32, 1: 16 * (idx % 8) + idx // 8}
    k_nat = {16: idx, 4: 4 * (idx % 32) + idx // 32, 1: idx}
    out = np.empty((3, 2, SWA_HEADS, BLK, 2 * BLK), np.float32)
    for pi, d in enumerate((16, 4, 1)):
        j = np.concatenate([k_nat[d], BLK + k_nat[d]])[None, :]
        stp = q_nat[d][:, None] + BLK - j
        band = (stp >= 0) & (stp <= BLK)
        for first in (0, 1):
            valid = band & ((j >= BLK) if first else True)
            bias = -LOG2E * slopes[:, None, None] * (stp * d).astype(np.float64)[None]
            out[pi, first] = np.where(valid[None], bias, NEG).astype(np.float32)
    return out


def _attend(q, kk, vv, bias_of, prev):
    lo = lax.broadcasted_iota(jnp.int32, (BLK, LANES), 1) < SWA_HD
    lo_k = lax.broadcasted_iota(jnp.int32, (2 * BLK, LANES), 1) < SWA_HD
    zero, one = jnp.zeros_like(q), jnp.ones_like(vv)
    q2 = jnp.concatenate([jnp.where(lo, q, zero), jnp.where(lo, zero, q)], axis=0)
    s2 = _dot_nt(q2, kk)
    m_new, res, alphas = [], [], []
    for hp in range(2):
        s = s2[hp * BLK:(hp + 1) * BLK] + bias_of(hp)
        mc = jnp.max(s, axis=1, keepdims=True)
        if prev is None:
            mn = jnp.broadcast_to(mc, (BLK, LANES))
        else:
            mn = jnp.maximum(prev[0][hp], mc)
            alphas.append(jnp.exp2(prev[0][hp] - mn))
        p = jnp.exp2(s - jnp.concatenate([mn, mn], axis=1)).astype(BF16)
        w = jnp.where(lo_k, vv, one) if hp == 0 else jnp.where(lo_k, one, vv)
        res.append(_dot(p, w))
        m_new.append(mn)
    acc = jnp.where(lo, res[0], res[1])
    l = jnp.where(lo, res[1], res[0])
    if prev is not None:
        acc = jnp.where(lo, alphas[0], alphas[1]) * prev[2] + acc
        l = jnp.where(lo, alphas[1], alphas[0]) * prev[1] + l
    return m_new, l, acc


def _attn_prompt_body(q_ref, kc_ref, vc_ref, kp_ref, vp_ref, ktc_ref, vtc_ref, ktp_ref, vtp_ref,
                      bias_ref, o_ref, m_scr, l_scr, a_scr, tk_scr, tv_scr):
    tile0 = (pl.program_id(2) == 0).astype(jnp.int32)
    tk_scr[0:BLK, :] = ktp_ref[...]
    tk_scr[BLK:, :] = ktc_ref[...]
    tv_scr[0:BLK, :] = vtp_ref[...]
    tv_scr[BLK:, :] = vtc_ref[...]

    for r in range(RES):
        kk = jnp.concatenate([kp_ref[r], kc_ref[r]], axis=0)
        vv = jnp.concatenate([vp_ref[r], vc_ref[r]], axis=0)
        m, l, a = _attend(q_ref[r].astype(BF16), kk, vv, lambda hp: bias_ref[0, tile0, hp], None)
        for hp in range(2):
            m_scr[hp, r] = m[hp]
        l_scr[r] = l
        a_scr[r] = a

    for r4 in range(4):
        for kb in range(4):
            qr = BLK // 4
            rows = slice(qr * kb, qr * kb + qr)

            def keys(cur, prev):
                if kb == 0:
                    before = [prev[4 * c + r4, BLK - qr:BLK, :] for c in range(4)]
                else:
                    before = [cur[4 * c + r4, qr * kb - qr:qr * kb, :] for c in range(4)]
                return jnp.concatenate(before + [cur[4 * c + r4, rows, :] for c in range(4)], axis=0)

            cat = lambda ref, *i: jnp.concatenate([ref[(*i, 4 * c + r4, rows)] for c in range(4)], axis=0)
            first = tile0 if kb == 0 else 0
            prev = ([cat(m_scr, hp) for hp in range(2)], cat(l_scr), cat(a_scr))
            m, l, a = _attend(cat(q_ref).astype(BF16), keys(kc_ref, kp_ref), keys(vc_ref, vp_ref),
                              lambda hp: bias_ref[1, first, hp], prev)
            for c in range(4):
                cs = slice(qr * c, qr * c + qr)
                for hp in range(2):
                    m_scr[hp, 4 * c + r4, rows, :] = m[hp][cs]
                l_scr[4 * c + r4, rows, :] = l[cs]
                a_scr[4 * c + r4, rows, :] = a[cs]

    for j in range(RES):
        rs = slice(j * SUBLANES, (j + 1) * SUBLANES)
        ks = slice(j * BLK, j * BLK + 2 * BLK)
        cat = lambda ref, *idx: ref[(*idx, slice(None), rs, slice(None))].reshape(BLK, LANES)
        first = tile0 if j == 0 else 0
        prev = ([cat(m_scr, hp) for hp in range(2)], cat(l_scr), cat(a_scr))
        m, l, a = _attend(cat(q_ref).astype(BF16), tk_scr[ks, :], tv_scr[ks, :],
                          lambda hp: bias_ref[2, first, hp], prev)
        for hp in range(2):
            m_scr[hp, :, rs, :] = m[hp].reshape(RES, SUBLANES, LANES)
        l_scr[:, rs, :] = l.reshape(RES, SUBLANES, LANES)
        a_scr[:, rs, :] = a.reshape(RES, SUBLANES, LANES)

    for r in range(RES):
        o_ref[r] = (a_scr[r] / pltpu.roll(l_scr[r], SWA_HD, 1)).astype(o_ref.dtype)


def _attn_prompt(q16, k16, v16, kt, vt):
    B, nt = q16.shape[0], q16.shape[1]
    bias = jnp.asarray(_prompt_bias_tables())
    tile = (None, None, RES, BLK, LANES)
    cur = pl.BlockSpec(tile, lambda g, b, a: (b, a, 0, 0, g))
    prv = pl.BlockSpec(tile, lambda g, b, a: (b, jnp.maximum(a - 1, 0), 0, 0, g))
    tcur = pl.BlockSpec((None, TILE, LANES), lambda g, b, a: (b, a, g))
    tprv = pl.BlockSpec((None, BLK, LANES),
                        lambda g, b, a: (b, jnp.maximum(a * (TILE // BLK) - 1, 0), g))
    bias_spec = pl.BlockSpec((3, 2, 2, BLK, 2 * BLK), lambda g, b, a: (0, 0, g, 0, 0))
    return pl.pallas_call(
        _attn_prompt_body,
        grid=(SWA_HEADS // 2, B, nt),
        in_specs=[cur, cur, cur, prv, prv, tcur, tcur, tprv, tprv, bias_spec],
        out_specs=cur,
        out_shape=jax.ShapeDtypeStruct((B, nt, RES, BLK, SW), BF16),
        scratch_shapes=[pltpu.VMEM((2, RES, BLK, LANES), F32),
                        pltpu.VMEM((RES, BLK, LANES), F32),
                        pltpu.VMEM((RES, BLK, LANES), F32),
                        pltpu.VMEM((TILE + BLK, LANES), BF16),
                        pltpu.VMEM((TILE + BLK, LANES), BF16)],
        compiler_params=_params(3),
        name="attn_prompt",
    )(q16, k16, v16, k16, v16, kt, vt, kt, vt, bias)


def _sample_tables(wb, ts):
    slopes = _alibi_slopes()
    ncol = wb + LANES
    t = np.arange(ts)[:, None]
    c = np.arange(ncol)[None, :]
    dist = wb + t - c
    mult = np.zeros((ts, ncol), np.float64)
    for (W, d) in DILATED_PATTERNS:
        mult += (dist >= 0) & (dist % d == 0) & (dist // d <= W // d) & (c < wb + ts)
    bias = np.where(mult[None] > 0, -slopes[:, None, None] * dist[None].astype(np.float64), NEG)
    mult = np.broadcast_to(mult[None], bias.shape)
    shp = (SWA_HEADS * ts, ncol)
    bias = bias.reshape(shp).astype(np.float32)
    mult = mult.reshape(shp).astype(np.float32)
    return bias[:, :wb], mult[:, :wb], bias[:, wb:], mult[:, wb:]


def _attn_sample_body(q_ref, kn_ref, vn_ref, kt_ref, vt_ref, bb_ref, mb_ref, bn_ref, mn_ref, o_ref,
                      *, ts):
    nr = SWA_HEADS * ts
    q = q_ref[...]
    qt = jnp.concatenate([q] * SWA_HEADS, axis=0)
    rh = lax.broadcasted_iota(jnp.int32, (nr, SW), 0) // ts
    ch = lax.broadcasted_iota(jnp.int32, (nr, SW), 1) // SWA_HD
    same = rh == ch
    qbd = jnp.where(same, qt, 0.0).astype(BF16)
    zpad = jnp.zeros((LANES - ts, SW), F32)
    kn = jnp.concatenate([kn_ref[...], zpad], axis=0).astype(BF16)
    vn = jnp.concatenate([vn_ref[...], zpad], axis=0).astype(BF16)
    s_b = _dot(qbd, kt_ref[...].astype(BF16)) + bb_ref[...]
    s_n = _dot_nt(qbd, kn) + bn_ref[...]
    m = jnp.maximum(jnp.max(s_b, axis=1, keepdims=True), jnp.max(s_n, axis=1, keepdims=True))
    p_b = jnp.exp(s_b - m) * mb_ref[...]
    p_n = jnp.exp(s_n - m) * mn_ref[...]
    den = jnp.sum(p_b, axis=1, keepdims=True) + jnp.sum(p_n, axis=1, keepdims=True)
    num = _dot_nt(p_b.astype(BF16), vt_ref[...].astype(BF16)) + _dot(p_n.astype(BF16), vn)
    full = jnp.where(same, num / den, 0.0)
    o = full[0:ts]
    for h in range(1, SWA_HEADS):
        o = o + full[h * ts:(h + 1) * ts]
    o_ref[...] = o


def _attn_sample_operands(qs, kn, vn, kbt, vbt, ts, seq_of):
    Bd, _, wb = kbt.shape
    for (W, d) in DILATED_PATTERNS:
        assert wb - (W // d) * d >= 0, "window buffer shorter than a pattern's reach"
    tables = [jnp.asarray(t) for t in _sample_tables(wb, ts)]
    row = pl.BlockSpec((ts, SW), lambda *ids: (seq_of(*ids), 0))
    buf = pl.BlockSpec((None, SW, wb), lambda *ids: (seq_of(*ids), 0, 0))
    operands = [qs, kn, vn, kbt, vbt] + tables
    specs = [row, row, row, buf, buf] + [_const_spec(t.shape) for t in tables]
    return operands, specs, row, jax.ShapeDtypeStruct((Bd * ts, SW), F32)


def _ffn_body(og_ref, os_ref, x_ref, wo_ref, wfg_ref, wfu_ref, wfd_ref, gf_ref, gl_ref, y_ref,
              *scratch, os_slabs):
    for blk in range(x_ref.shape[0] // ROWS):
        rows = slice(blk * ROWS, (blk + 1) * ROWS)
        if os_slabs:
            stage, stage4 = scratch
            sub = slice(blk * SUB, (blk + 1) * SUB)
            for sl in range(SW // LANES):
                for r4 in range(4):
                    for c in range(4):
                        stage4[blk, sl, r4, pl.ds(c, SUB, stride=4), :] = (
                            os_ref[4 * c + r4, sub, sl * LANES:(sl + 1) * LANES].astype(F32))
                for r4 in range(4):
                    stage[blk, sl, pl.ds(r4, ROWS // 4, stride=4), :] = stage4[blk, sl, r4]
            osw = jnp.concatenate([stage[blk, sl] for sl in range(SW // LANES)],
                                  axis=1).astype(BF16)
        else:
            osw = os_ref[rows, :].astype(BF16)
        og = og_ref[rows, :].astype(BF16)
        h = x_ref[rows, :] + _dot(og, wo_ref[0:GV, :]) + _dot(osw, wo_ref[GV:GV + SW, :])
        hn = _rms(h, gf_ref[...]).astype(BF16)
        acc = jnp.zeros(h.shape, F32)
        for c in range(D_FF // FF_CHUNK):
            cs = slice(c * FF_CHUNK, (c + 1) * FF_CHUNK)
            gate = _dot(hn, wfg_ref[:, cs])
            up = _dot(hn, wfu_ref[:, cs])
            act = (gate * _sigmoid(gate) * up).astype(BF16)
            acc = acc + _dot(act, wfd_ref[cs, :])
        y_ref[rows, :] = _rms(h + acc, gl_ref[...])


def _ffn_weights(w_out, g_ffn, w_fg, w_fu, w_fd, g_final):
    return (w_out.astype(BF16), w_fg.astype(BF16), w_fu.astype(BF16), w_fd.astype(BF16),
            g_ffn.reshape(1, D_MODEL), g_final.reshape(1, D_MODEL))


def _ffn_prompt(og, o16, x, wts):
    B, T, _ = x.shape
    nblk = FFN_BLOCKS
    step = nblk * ROWS
    per = TILE // step
    wo, wfg, wfu, wfd, gf, gl = wts
    tok = lambda c: pl.BlockSpec((None, step, c), lambda b, i: (b, i, 0))
    slab = pl.BlockSpec((None, None, RES, nblk * SUB, SW),
                        lambda b, i: (b, i // per, 0, i % per, 0))
    return pl.pallas_call(
        functools.partial(_ffn_body, os_slabs=True),
        grid=(B, T // step),
        in_specs=[tok(GV), slab, tok(D_MODEL),
                  _resident(wo.shape), _resident(wfg.shape), _resident(wfu.shape),
                  _resident(wfd.shape), _const_spec(gf.shape), _const_spec(gl.shape)],
        out_specs=tok(D_MODEL),
        out_shape=jax.ShapeDtypeStruct((B, T, D_MODEL), F32),
        scratch_shapes=[pltpu.VMEM((nblk, SW // LANES, ROWS, LANES), F32),
                        pltpu.VMEM((nblk, SW // LANES, 4, ROWS // 4, LANES), F32)],
        compiler_params=_params(2),
        name="ffn_prompt",
    )(og, o16, x, wo, wfg, wfu, wfd, gf, gl)


def _ffn_sample(og, osw, x, wts):
    N = x.shape[0]
    wo, wfg, wfu, wfd, gf, gl = wts
    spec = lambda c: pl.BlockSpec((ROWS, c), lambda i: (i, 0))
    return pl.pallas_call(
        functools.partial(_ffn_body, os_slabs=False),
        grid=(N // ROWS,),
        in_specs=[spec(GV), spec(SW), spec(D_MODEL),
                  _resident(wo.shape), _resident(wfg.shape), _resident(wfu.shape),
                  _resident(wfd.shape), _const_spec(gf.shape), _const_spec(gl.shape)],
        out_specs=spec(D_MODEL),
        out_shape=jax.ShapeDtypeStruct((N, D_MODEL), F32),
        compiler_params=_params(1),
        name="ffn_sample",
    )(og, osw, x, wo, wfg, wfu, wfd, gf, gl)


def _layer(xp, xs, s0, kbuf, vbuf, w_in, w_gup, b_gate, g_mix, g_gla, w_out, g_ffn,
           w_fg, w_fu, w_fd, g_last):
    B, T, _ = xp.shape
    Bd, Ts, _ = xs.shape
    N = Bd * Ts
    assert T % TILE == 0 and T >= WIN_MAX and WIN_MAX % ROWS == 0
    assert CHUNK % Ts == 0 and N % ROWS == 0
    w, wg, bg, gm = _pack_weights(w_in, w_gup, b_gate, g_mix)
    wts = _ffn_weights(w_out, g_ffn, w_fg, w_fu, w_fd, g_last)
    xs2 = xs.reshape(N, D_MODEL)

    qg, kg, vg, r, la, qs, ks, vs = _inproj_sample(xs2, w, wg, bg, gm)
    ogs, ss = _gla_sample(qg, kg, vg, la, r, g_gla, s0, Ts)
    wb = kbuf.shape[1]
    feat = lambda a: jnp.transpose(a, (0, 2, 3, 1)).reshape(Bd, SW, wb)

    qg, kg, vg, r, la, q16, k16, v16, kt, vt, kk, vk = _inproj_prompt(xp, w, wg, bg, gm)
    og, sp, oss = _gla_prompt(qg, kg, vg, la, r, g_gla, (qs, ks, vs, feat(kbuf), feat(vbuf), Ts))
    o16 = _attn_prompt(q16, k16, v16, kt, vt)
    yp = _ffn_prompt(og, o16, xp, wts)
    ys = _ffn_sample(ogs, oss, xs2, wts)
    keep = lambda a: jnp.transpose(a.reshape(B, SWA_HEADS, SWA_HD, WIN_MAX), (0, 3, 1, 2))
    new = lambda a: a.reshape(Bd, Ts, SWA_HEADS, SWA_HD)
    return yp, ys.reshape(Bd, Ts, D_MODEL), sp, ss, keep(kk), keep(vk), new(ks), new(vs)


def kernel(x_prompt, x_sample, state_gla, cache_swa_k, cache_swa_v, w_in, w_gate_up, b_gate,
           g_mix_norm, g_gla_norm, w_out, g_ffn_norm, w_ffn_gate, w_ffn_up, w_ffn_down, g_final):
    depth = w_in.shape[0]
    assert depth == 1, "the final norm is fused into the single layer's FFN kernel"
    outs = _layer(x_prompt, x_sample, state_gla[0], cache_swa_k[0], cache_swa_v[0],
                  w_in[0], w_gate_up[0], b_gate[0], g_mix_norm[0], g_gla_norm[0], w_out[0],
                  g_ffn_norm[0], w_ffn_gate[0], w_ffn_up[0], w_ffn_down[0], g_final)
    yp, ys = outs[:2]
    return (yp, ys) + tuple(o[None] for o in outs[2:])
```
